```python
import math
import jax, jax.numpy as jnp
from jax import lax
import numpy as np

D_MODEL = 1024
BATCH = 8
SEQ = 2048
DEPTH = 4

N_MIXERS = 3
HEAD_DIM = 64
MIX_WIDTH = 768
N_MIX_HEADS = MIX_WIDTH // HEAD_DIM
MEM_LEN = 256
N_MEM_HEADS = 4
MEM_WIDTH = N_MEM_HEADS * HEAD_DIM
MIXING_WIDTH = MIX_WIDTH + MEM_WIDTH
Q_BLOCK = 128
POOL_WINDOWS = (2, 4, 8, 16)
N_POOL_GROUPS = len(POOL_WINDOWS)
POOL_GROUP_DIM = MIX_WIDTH // N_POOL_GROUPS
Q_LORA_RANK = 384
KV_LORA_RANK = 256
QK_NOPE_DIM = 64
QK_ROPE_DIM = 32
V_HEAD_DIM = 64
ROPE_THETA = 10000.0
D_FF = 2816
ALPHA = (2 * DEPTH) ** 0.25
BETA = (8 * DEPTH) ** -0.25
LN_EPS = 1e-5
RMS_EPS = 1e-6
N_FOX = (DEPTH + 2) // 3
N_POOL = (DEPTH + 1) // 3
N_MLA = DEPTH // 3
FOX_IN = 3 * MIX_WIDTH + N_MIX_HEADS + MEM_WIDTH
POOL_IN = MIX_WIDTH + MEM_WIDTH
MLA_IN = Q_LORA_RANK + KV_LORA_RANK + QK_ROPE_DIM + MEM_WIDTH

kernel_name = "hybrid_fox_pool_mla_macaron_deepnorm"

F32 = jnp.float32


def layer_norm(x, g, b):
    xf = x.astype(F32)
    mu = jnp.mean(xf, axis=-1, keepdims=True)
    var = jnp.mean(jnp.square(xf - mu), axis=-1, keepdims=True)
    return ((xf - mu) * lax.rsqrt(var + LN_EPS) * g.astype(F32) + b.astype(F32)).astype(x.dtype)


def rms_norm(x, g):
    xf = x.astype(F32)
    ms = jnp.mean(jnp.square(xf), axis=-1, keepdims=True)
    return (xf * lax.rsqrt(ms + RMS_EPS) * g.astype(F32)).astype(x.dtype)


def swiglu(x, w_gate, w_up, w_down):
    return (jax.nn.silu(x @ w_gate) * (x @ w_up)) @ w_down


def split_heads(t, n_heads):
    b, s, _ = t.shape
    return t.reshape(b, s, n_heads, -1).transpose(0, 2, 1, 3)


def merge_heads(t):
    b, h, s, d = t.shape
    return t.transpose(0, 2, 1, 3).reshape(b, s, h * d)


def rope(x, cos, sin):
    x1, x2 = jnp.split(x, 2, axis=-1)
    return jnp.concatenate([x1 * cos - x2 * sin, x1 * sin + x2 * cos], axis=-1)


def causal_block_attention(q, k, v, decay=None):
    b, h, s_len, dk = q.shape
    dv = v.shape[-1]
    nb = s_len // Q_BLOCK
    scale = dk ** -0.5
    k_pos = jnp.arange(s_len)
    q_blocks = q.reshape(b, h, nb, Q_BLOCK, dk).transpose(2, 0, 1, 3, 4)
    if decay is None:
        xs = (jnp.arange(nb), q_blocks)
    else:
        xs = (jnp.arange(nb), q_blocks, decay.reshape(b, h, nb, Q_BLOCK).transpose(2, 0, 1, 3))

    def body(args):
        i, q_i = args[0], args[1]
        s = jnp.einsum('bhqd,bhkd->bhqk', q_i, k).astype(F32) * scale
        if decay is not None:
            s = s + args[2][..., :, None] - decay[..., None, :]
        q_pos = i * Q_BLOCK + jnp.arange(Q_BLOCK)
        s = jnp.where(k_pos[None, :] <= q_pos[:, None], s, -jnp.inf)
        p = jax.nn.softmax(s, axis=-1)
        return jnp.einsum('bhqk,bhkd->bhqd', p.astype(v.dtype), v)

    out = lax.map(body, xs)
    return out.transpose(1, 2, 0, 3, 4).reshape(b, h, s_len, dv)


def memory_attention(q, mem, w_kv):
    k, v = jnp.split(mem @ w_kv, 2, axis=-1)
    qh, kh, vh = (split_heads(t, N_MEM_HEADS) for t in (q, k, v))
    s = jnp.einsum('bhqd,bhkd->bhqk', qh, kh).astype(F32) * (HEAD_DIM ** -0.5)
    p = jax.nn.softmax(s, axis=-1)
    return merge_heads(jnp.einsum('bhqk,bhkd->bhqd', p.astype(vh.dtype), vh))


def fox_mixer(h, b_f):
    q, k, v, f_logit = jnp.split(h, [MIX_WIDTH, 2 * MIX_WIDTH, 3 * MIX_WIDTH], axis=-1)
    log_f = jax.nn.log_sigmoid((f_logit + b_f).astype(F32))
    decay = jnp.cumsum(log_f, axis=1).transpose(0, 2, 1)
    o = causal_block_attention(split_heads(q, N_MIX_HEADS), split_heads(k, N_MIX_HEADS),
                               split_heads(v, N_MIX_HEADS), decay)
    return merge_heads(o)


def pool_mixer(h, w_grp, scale):
    b, s_len, _ = h.shape
    u = h.reshape(b, s_len, N_POOL_GROUPS, POOL_GROUP_DIM)
    csum = jnp.cumsum(u.astype(F32), axis=1)
    t = jnp.arange(s_len)
    means = []
    for g, w in enumerate(POOL_WINDOWS):
        c = csum[:, :, g]
        c_prev = jnp.pad(c, ((0, 0), (w, 0), (0, 0)))[:, :s_len]
        count = jnp.minimum(t + 1, w).astype(F32)
        means.append((c - c_prev) / count[None, :, None])
    pooled = jnp.stack(means, axis=2).astype(h.dtype) - u
    y = jnp.einsum('bsgc,gcd->bsgd', pooled, w_grp)
    return y.reshape(b, s_len, MIX_WIDTH) * scale


def mla_mixer(h, cos, sin, q_norm, kv_norm, w_uq, w_ukv):
    b, s_len, _ = h.shape
    c_q, c_kv, k_rope = jnp.split(h, [Q_LORA_RANK, Q_LORA_RANK + KV_LORA_RANK], axis=-1)
    q = split_heads(rms_norm(c_q, q_norm) @ w_uq, N_MIX_HEADS)
    kv = split_heads(rms_norm(c_kv, kv_norm) @ w_ukv, N_MIX_HEADS)
    q_nope, q_pe = jnp.split(q, [QK_NOPE_DIM], axis=-1)
    k_nope, v = jnp.split(kv, [QK_NOPE_DIM], axis=-1)
    q_pe = rope(q_pe, cos[:, None], sin[:, None])
    k_pe = jnp.broadcast_to(rope(k_rope, cos, sin)[:, None],
                            (b, N_MIX_HEADS, s_len, QK_ROPE_DIM))
    q = jnp.concatenate([q_nope, q_pe], axis=-1)
    k = jnp.concatenate([k_nope, k_pe], axis=-1)
    return merge_heads(causal_block_attention(q, k, v))


def setup_inputs(seed: int = 0) -> dict:
    key = jax.random.key(seed)
    ks = jax.random.split(key, 20)

    def nrm(k, shape, std):
        return std * jax.random.normal(k, shape, F32)

    x = jax.random.normal(ks[0], (BATCH, SEQ, D_MODEL), F32)
    mem = jax.random.normal(ks[1], (BATCH, MEM_LEN, D_MODEL), F32)
    start = jax.random.randint(ks[2], (BATCH, 1), 0, 4096, dtype=jnp.int32)
    positions = (start + jnp.arange(SEQ, dtype=jnp.int32)[None, :]).astype(jnp.int32)
    ln_g = 1.0 + nrm(ks[3], (DEPTH, 3, D_MODEL), 0.02)
    ln_b = nrm(ks[4], (DEPTH, 3, D_MODEL), 0.02)
    ffn_w_gate = nrm(ks[5], (DEPTH, 2, D_MODEL, D_FF), D_MODEL ** -0.5)
    ffn_w_up = nrm(ks[6], (DEPTH, 2, D_MODEL, D_FF), D_MODEL ** -0.5)
    ffn_w_down = nrm(ks[7], (DEPTH, 2, D_FF, D_MODEL), BETA * D_FF ** -0.5)
    mem_w_kv = nrm(ks[8], (DEPTH, D_MODEL, 2 * MEM_WIDTH), D_MODEL ** -0.5)
    w_o = nrm(ks[9], (DEPTH, MIXING_WIDTH, D_MODEL), BETA * MIXING_WIDTH ** -0.5)
    fox_w_in = nrm(ks[10], (N_FOX, D_MODEL, FOX_IN), D_MODEL ** -0.5)
    fox_b_f = jax.random.uniform(ks[11], (N_FOX, N_MIX_HEADS), F32, 1.0, 4.0)
    pool_w_in = nrm(ks[12], (N_POOL, D_MODEL, POOL_IN), D_MODEL ** -0.5)
    pool_w_grp = nrm(ks[13], (N_POOL, N_POOL_GROUPS, POOL_GROUP_DIM, POOL_GROUP_DIM),
                     POOL_GROUP_DIM ** -0.5)
    pool_scale = 1.0 + nrm(ks[14], (N_POOL, MIX_WIDTH), 0.02)
    mla_w_in = nrm(ks[15], (N_MLA, D_MODEL, MLA_IN), D_MODEL ** -0.5)
    mla_q_norm = 1.0 + nrm(ks[16], (N_MLA, Q_LORA_RANK), 0.02)
    mla_kv_norm = 1.0 + nrm(ks[17], (N_MLA, KV_LORA_RANK), 0.02)
    mla_w_uq = nrm(ks[18], (N_MLA, Q_LORA_RANK, N_MIX_HEADS * (QK_NOPE_DIM + QK_ROPE_DIM)),
                   Q_LORA_RANK ** -0.5)
    mla_w_ukv = nrm(ks[19], (N_MLA, KV_LORA_RANK, N_MIX_HEADS * (QK_NOPE_DIM + V_HEAD_DIM)),
                    KV_LORA_RANK ** -0.5)
    return {"x": x, "mem": mem, "positions": positions, "ln_g": ln_g, "ln_b": ln_b,
            "ffn_w_gate": ffn_w_gate, "ffn_w_up": ffn_w_up, "ffn_w_down": ffn_w_down,
            "mem_w_kv": mem_w_kv, "w_o": w_o, "fox_w_in": fox_w_in, "fox_b_f": fox_b_f,
            "pool_w_in": pool_w_in, "pool_w_grp": pool_w_grp, "pool_scale": pool_scale,
            "mla_w_in": mla_w_in, "mla_q_norm": mla_q_norm, "mla_kv_norm": mla_kv_norm,
            "mla_w_uq": mla_w_uq, "mla_w_ukv": mla_w_ukv}


def reference(x, mem, positions, ln_g, ln_b, ffn_w_gate, ffn_w_up, ffn_w_down, mem_w_kv,
              w_o, fox_w_in, fox_b_f, pool_w_in, pool_w_grp, pool_scale, mla_w_in,
              mla_q_norm, mla_kv_norm, mla_w_uq, mla_w_ukv):
    inv_freq = ROPE_THETA ** (-jnp.arange(0, QK_ROPE_DIM, 2, dtype=F32) / QK_ROPE_DIM)
    ang = positions.astype(F32)[..., None] * inv_freq
    cos = jnp.cos(ang).astype(x.dtype)
    sin = jnp.sin(ang).astype(x.dtype)

    for i in range(DEPTH):
        kind, j = i % N_MIXERS, i // N_MIXERS
        ff = swiglu(x, ffn_w_gate[i, 0], ffn_w_up[i, 0], ffn_w_down[i, 0])
        x = layer_norm(ALPHA * x + 0.5 * ff, ln_g[i, 0], ln_b[i, 0])
        if kind == 0:
            h = x @ fox_w_in[j]
            main = fox_mixer(h[..., :-MEM_WIDTH], fox_b_f[j])
        elif kind == 1:
            h = x @ pool_w_in[j]
            main = pool_mixer(h[..., :-MEM_WIDTH], pool_w_grp[j], pool_scale[j])
        else:
            h = x @ mla_w_in[j]
            main = mla_mixer(h[..., :-MEM_WIDTH], cos, sin, mla_q_norm[j], mla_kv_norm[j],
                             mla_w_uq[j], mla_w_ukv[j])
        mem_out = memory_attention(h[..., -MEM_WIDTH:], mem, mem_w_kv[i])
        mix = jnp.concatenate([main, mem_out], axis=-1) @ w_o[i]
        x = layer_norm(ALPHA * x + mix, ln_g[i, 1], ln_b[i, 1])
        ff = swiglu(x, ffn_w_gate[i, 1], ffn_w_up[i, 1], ffn_w_down[i, 1])
        x = layer_norm(ALPHA * x + 0.5 * ff, ln_g[i, 2], ln_b[i, 2])
    return x
```

```python
import functools

import jax
import jax.numpy as jnp
from jax import lax
from jax.experimental import pallas as pl
from jax.experimental.pallas import tpu as pltpu

F32 = jnp.float32
BF16 = jnp.bfloat16

N_MIXERS = 3
HEAD_DIM = 64
MIX_WIDTH = 768
N_MIX_HEADS = MIX_WIDTH // HEAD_DIM
N_MEM_HEADS = 4
MEM_WIDTH = N_MEM_HEADS * HEAD_DIM
POOL_WINDOWS = (2, 4, 8, 16)
POOL_GROUP_DIM = MIX_WIDTH // len(POOL_WINDOWS)
Q_LORA_RANK = 384
KV_LORA_RANK = 256
QK_NOPE_DIM = 64
QK_ROPE_DIM = 32
V_HEAD_DIM = 64
ROPE_THETA = 10000.0
LN_EPS = 1e-5
RMS_EPS = 1e-6

LANES = 128
VMEM_LIMIT_BYTES = 56 * 1024 * 1024

FFN_TM = 512
FFN_TF = 256
PROJ_TM = 512
ATT_TQ = 256
ATT_TK = 256
NEG_BIG = -1e30


def _cparams(sem):
    return pltpu.CompilerParams(dimension_semantics=sem, vmem_limit_bytes=VMEM_LIMIT_BYTES)


def _const_spec(shape):
    nd = len(shape)
    return pl.BlockSpec(shape, lambda *_: (0,) * nd, pipeline_mode=pl.Buffered(1))


def _layer_norm(y, g, b):
    mu = jnp.mean(y, axis=-1, keepdims=True)
    d = y - mu
    var = jnp.mean(d * d, axis=-1, keepdims=True)
    return d * lax.rsqrt(var + LN_EPS) * g + b


def _rms_norm(y, g):
    ms = jnp.mean(y * y, axis=-1, keepdims=True)
    return y * lax.rsqrt(ms + RMS_EPS) * g


def _ffn_kernel(x_ref, wgu_ref, wd_ref, g_ref, b_ref, o_ref, acc_ref, *, alpha, n_chunks, tf):
    x = x_ref[...]
    xb = x.astype(BF16)

    def chunk(c):
        gu = jnp.dot(xb, wgu_ref[c], preferred_element_type=F32)
        gate = gu[:, :tf]
        up = gu[:, tf:]
        h = (gate * jax.nn.sigmoid(gate)) * up
        return jnp.dot(h.astype(BF16), wd_ref[c], preferred_element_type=F32)

    acc_ref[...] = chunk(0)

    def body(c, carry):
        acc_ref[...] += chunk(c)
        return carry

    lax.fori_loop(1, n_chunks, body, 0)
    y = alpha * x + 0.5 * acc_ref[...]
    o_ref[...] = _layer_norm(y, g_ref[...], b_ref[...])


def _ffn_ln(x2d, wgu, wd, g, b, alpha):
    t, d = x2d.shape
    n_chunks, _, tf2 = wgu.shape
    tf = tf2 // 2
    tm = FFN_TM
    return pl.pallas_call(
        functools.partial(_ffn_kernel, alpha=alpha, n_chunks=n_chunks, tf=tf),
        out_shape=jax.ShapeDtypeStruct((t, d), F32),
        grid=(t // tm,),
        in_specs=[
            pl.BlockSpec((tm, d), lambda i: (i, 0)),
            _const_spec(wgu.shape),
            _const_spec(wd.shape),
            _const_spec((1, d)),
            _const_spec((1, d)),
        ],
        out_specs=pl.BlockSpec((tm, d), lambda i: (i, 0)),
        scratch_shapes=[pltpu.VMEM((tm, d), F32)],
        compiler_params=_cparams(("arbitrary",)),
        name="ffn_ln",
    )(x2d, wgu, wd, g, b)


def _memkv_kernel(m_ref, w_ref, o_ref):
    o_ref[0] = jnp.dot(m_ref[0].astype(BF16), w_ref[...], preferred_element_type=F32).astype(BF16)


def _mem_kv(mem, w_all):
    b, m, d = mem.shape
    n = w_all.shape[1]
    return pl.pallas_call(
        _memkv_kernel,
        out_shape=jax.ShapeDtypeStruct((b, m, n), BF16),
        grid=(b,),
        in_specs=[pl.BlockSpec((1, m, d), lambda i: (i, 0, 0)), _const_spec(w_all.shape)],
        out_specs=pl.BlockSpec((1, m, n), lambda i: (i, 0, 0)),
        compiler_params=_cparams(("arbitrary",)),
        name="mem_kv",
    )(mem, w_all)


def _fox_proj_kernel(x_ref, w_ref, bf_ref, q_ref, k_ref, v_ref, dcol_ref, drow_ref, mq_ref,
                     carry_ref, *, tm):
    si = pl.program_id(1)
    xb = x_ref[0].astype(BF16)
    h = jnp.dot(xb, w_ref[...], preferred_element_type=F32)
    mw = MIX_WIDTH
    q_ref[0] = h[:, :mw].astype(BF16)
    k_ref[0] = h[:, mw:2 * mw].astype(BF16)
    v_ref[0] = h[:, 2 * mw:3 * mw].astype(BF16)
    mq_ref[0] = h[:, 3 * mw + LANES:].astype(BF16)

    lf = jax.nn.log_sigmoid(h[:, 3 * mw:3 * mw + LANES] + bf_ref[...])
    row = lax.broadcasted_iota(jnp.int32, (tm, LANES), 0)
    shift = 1
    while shift < tm:
        lf = lf + jnp.where(row >= shift, pltpu.roll(lf, shift, 0), 0.0)
        shift *= 2

    @pl.when(si == 0)
    def _():
        carry_ref[...] = jnp.zeros_like(carry_ref)

    cum = lf + carry_ref[0:1, :]
    carry_ref[...] = jnp.broadcast_to(cum[tm - 1:tm, :], carry_ref.shape)
    dcol_ref[0] = cum
    drow_ref[0] = cum.T[:drow_ref.shape[1], :]


def _fox_proj(x, w, bf_pad):
    b, s, d = x.shape
    tm = PROJ_TM
    n = w.shape[1]
    hr = 16
    outs = (
        jax.ShapeDtypeStruct((b, s, MIX_WIDTH), BF16),
        jax.ShapeDtypeStruct((b, s, MIX_WIDTH), BF16),
        jax.ShapeDtypeStruct((b, s, MIX_WIDTH), BF16),
        jax.ShapeDtypeStruct((b, s, LANES), F32),
        jax.ShapeDtypeStruct((b, hr, s), F32),
        jax.ShapeDtypeStruct((b, s, MEM_WIDTH), BF16),
    )
    tok = lambda w_: pl.BlockSpec((1, tm, w_), lambda bi, si: (bi, si, 0))
    return pl.pallas_call(
        functools.partial(_fox_proj_kernel, tm=tm),
        out_shape=outs,
        grid=(b, s // tm),
        in_specs=[tok(d), _const_spec((d, n)), _const_spec((1, LANES))],
        out_specs=(tok(MIX_WIDTH), tok(MIX_WIDTH), tok(MIX_WIDTH), tok(LANES),
                   pl.BlockSpec((1, hr, tm), lambda bi, si: (bi, 0, si)), tok(MEM_WIDTH)),
        scratch_shapes=[pltpu.VMEM((8, LANES), F32)],
        compiler_params=_cparams(("arbitrary", "arbitrary")),
        name="fox_proj",
    )(x, w, bf_pad)


def _pool_kernel(x_ref, w_ref, wg_ref, sc_ref, main_ref, mq_ref, carry_ref, *, tm):
    si = pl.program_id(1)
    halo = carry_ref.shape[0]
    xb = x_ref[0].astype(BF16)
    h = jnp.dot(xb, w_ref[...], preferred_element_type=F32)
    u = h[:, :MIX_WIDTH]
    mq_ref[0] = h[:, MIX_WIDTH:].astype(BF16)

    @pl.when(si == 0)
    def _():
        carry_ref[...] = jnp.zeros_like(carry_ref)

    ext = jnp.concatenate([carry_ref[...], u], axis=0)
    carry_ref[...] = u[tm - halo:, :]

    lane = lax.broadcasted_iota(jnp.int32, (1, MIX_WIDTH), 1)
    t_pos = si * tm + lax.broadcasted_iota(jnp.int32, (tm, 1), 0)
    acc = ext
    pooled = None
    span = 1
    for gi, win in enumerate(POOL_WINDOWS):
        while span < win:
            acc = acc + pltpu.roll(acc, span, 0)
            span *= 2
        cnt = jnp.minimum(t_pos + 1, win).astype(F32)
        mean = acc[halo:, :] / cnt
        if pooled is None:
            pooled = mean
        else:
            pooled = jnp.where(lane >= gi * POOL_GROUP_DIM, mean, pooled)
    pooled = pooled - u
    y = jnp.dot(pooled.astype(BF16), wg_ref[...], preferred_element_type=F32)
    main_ref[0] = (y * sc_ref[...]).astype(BF16)


def _pool_proj(x, w, w_bd, scale):
    b, s, d = x.shape
    tm = PROJ_TM
    halo = 16
    tok = lambda w_: pl.BlockSpec((1, tm, w_), lambda bi, si: (bi, si, 0))
    return pl.pallas_call(
        functools.partial(_pool_kernel, tm=tm),
        out_shape=(jax.ShapeDtypeStruct((b, s, MIX_WIDTH), BF16),
                   jax.ShapeDtypeStruct((b, s, MEM_WIDTH), BF16)),
        grid=(b, s // tm),
        in_specs=[tok(d), _const_spec(w.shape), _const_spec(w_bd.shape),
                  _const_spec((1, MIX_WIDTH))],
        out_specs=(tok(MIX_WIDTH), tok(MEM_WIDTH)),
        scratch_shapes=[pltpu.VMEM((halo, MIX_WIDTH), F32)],
        compiler_params=_cparams(("arbitrary", "arbitrary")),
        name="pool_proj",
    )(x, w, w_bd, scale)


def _rope(xv, cos_t, sin_t, lane):
    half = QK_ROPE_DIM // 2
    swapped = jnp.where((lane % QK_ROPE_DIM) < half,
                        pltpu.roll(xv, LANES - half, 1), pltpu.roll(xv, half, 1))
    return xv * cos_t + swapped * sin_t


def _mla_proj_kernel(x_ref, w_ref, qn_ref, kvn_ref, wuq_ref, wukv_ref, cos_ref, sin_ref,
                     qnope_ref, qpe_ref, knope_ref, kpe_ref, v_ref, mq_ref):
    xb = x_ref[0].astype(BF16)
    h = jnp.dot(xb, w_ref[...], preferred_element_type=F32)
    o1 = Q_LORA_RANK
    o2 = o1 + KV_LORA_RANK
    o3 = o2 + LANES
    cq = _rms_norm(h[:, :o1], qn_ref[...])
    ckv = _rms_norm(h[:, o1:o2], kvn_ref[...])
    mq_ref[0] = h[:, o3:].astype(BF16)
    cos_t = cos_ref[0]
    sin_t = sin_ref[0]
    lane = lax.broadcasted_iota(jnp.int32, (1, LANES), 1)
    kpe_ref[0] = _rope(h[:, o2:o3], cos_t, sin_t, lane).astype(BF16)

    q = jnp.dot(cq.astype(BF16), wuq_ref[...], preferred_element_type=F32)
    qnope_ref[0] = q[:, :MIX_WIDTH].astype(BF16)
    pe_w = N_MIX_HEADS * QK_ROPE_DIM
    for blk in range(pe_w // LANES):
        lo = MIX_WIDTH + blk * LANES
        qpe_ref[0, :, blk * LANES:(blk + 1) * LANES] = _rope(
            q[:, lo:lo + LANES], cos_t, sin_t, lane).astype(BF16)
    kv = jnp.dot(ckv.astype(BF16), wukv_ref[...], preferred_element_type=F32)
    knope_ref[0] = kv[:, :MIX_WIDTH].astype(BF16)
    v_ref[0] = kv[:, MIX_WIDTH:].astype(BF16)


def _mla_proj(x, w, qn, kvn, wuq, wukv, cos_t, sin_t):
    b, s, d = x.shape
    tm = PROJ_TM
    pe_w = N_MIX_HEADS * QK_ROPE_DIM
    tok = lambda w_: pl.BlockSpec((1, tm, w_), lambda bi, si: (bi, si, 0))
    outs = tuple(jax.ShapeDtypeStruct((b, s, w_), BF16)
                 for w_ in (MIX_WIDTH, pe_w, MIX_WIDTH, LANES, MIX_WIDTH, MEM_WIDTH))
    return pl.pallas_call(
        _mla_proj_kernel,
        out_shape=outs,
        grid=(b, s // tm),
        in_specs=[tok(d), _const_spec(w.shape), _const_spec(qn.shape), _const_spec(kvn.shape),
                  _const_spec(wuq.shape), _const_spec(wukv.shape), tok(LANES), tok(LANES)],
        out_specs=(tok(MIX_WIDTH), tok(pe_w), tok(MIX_WIDTH), tok(LANES), tok(MIX_WIDTH),
                   tok(MEM_WIDTH)),
        compiler_params=_cparams(("arbitrary", "arbitrary")),
        name="mla_proj",
    )(x, w, qn, kvn, wuq, wukv, cos_t, sin_t)


def _flash_kernel(*refs, tq, tk, scale, has_decay, has_rope):
    it = iter(refs)
    q_ref, k_ref, v_ref = next(it), next(it), next(it)
    if has_decay:
        dcol_ref, drow_ref = next(it), next(it)
    if has_rope:
        qpe_ref, kpe_ref = next(it), next(it)
    o_ref = next(it)

    j = pl.program_id(1)
    qi = pl.program_id(2)
    lane = lax.broadcasted_iota(jnp.int32, (1, LANES), 1)
    q2 = q_ref[0]
    rows = qi * tq + lax.broadcasted_iota(jnp.int32, (tq, 1), 0)
    n_kv = (qi * tq + tq + tk - 1) // tk
    outs = []
    for hh in range(2):
        hmask = (lane // HEAD_DIM) == hh
        qm = jnp.where(hmask, q2, jnp.zeros_like(q2))
        hidx = 2 * j + hh
        if has_decay:
            dq = jnp.sum(jnp.where(lane == hidx, dcol_ref[0], 0.0), axis=-1, keepdims=True)
        if has_rope:
            qpe = qpe_ref[0]
            grp = 2 * (j % 2) + hh
            qpem = jnp.where((lane // QK_ROPE_DIM) == grp, qpe, jnp.zeros_like(qpe))

        def body(kb, carry):
            m, l, acc = carry
            ks = pl.multiple_of(kb * tk, tk)
            k2 = k_ref[0, pl.ds(ks, tk), :]
            v2 = v_ref[0, pl.ds(ks, tk), :]
            s = lax.dot_general(qm, k2, (((1,), (1,)), ((), ())), preferred_element_type=F32)
            if has_rope:
                kpe = kpe_ref[0, pl.ds(ks, tk), :]
                s = s + lax.dot_general(qpem, kpe, (((1,), (1,)), ((), ())),
                                        preferred_element_type=F32)
            if scale != 1.0:
                s = s * scale
            if has_decay:
                dk = drow_ref[0, pl.ds(hidx, 1), pl.ds(ks, tk)]
                s = s + dq - dk
            cols = ks + lax.broadcasted_iota(jnp.int32, (1, tk), 1)
            s = jnp.where(cols <= rows, s, -jnp.inf)
            m_new = jnp.maximum(m, jnp.max(s, axis=-1, keepdims=True))
            a = jnp.exp(m - m_new)
            p = jnp.exp(s - m_new)
            l_new = a * l + jnp.sum(p, axis=-1, keepdims=True)
            acc_new = a * acc + jnp.dot(p.astype(BF16), v2, preferred_element_type=F32)
            return m_new, l_new, acc_new

        init = (jnp.full((tq, 1), NEG_BIG, F32), jnp.zeros((tq, 1), F32),
                jnp.zeros((tq, LANES), F32))
        m, l, acc = lax.fori_loop(0, n_kv, body, init)
        outs.append(acc / l)
    o_ref[0] = jnp.where((lane // HEAD_DIM) == 0, outs[0], outs[1]).astype(BF16)


def _flash(q, k, v, *, scale, decay=None, rope=None):
    b, s, _ = q.shape
    tq, tk = ATT_TQ, ATT_TK
    n_pairs = N_MIX_HEADS // 2
    qspec = pl.BlockSpec((1, tq, LANES), lambda bi, j, qi: (bi, qi, j))
    kvspec = pl.BlockSpec((1, s, LANES), lambda bi, j, qi: (bi, 0, j))
    args = [q, k, v]
    specs = [qspec, kvspec, kvspec]
    if decay is not None:
        dcol, drow = decay
        args += [dcol, drow]
        specs += [pl.BlockSpec((1, tq, LANES), lambda bi, j, qi: (bi, qi, 0)),
                  pl.BlockSpec((1, drow.shape[1], s), lambda bi, j, qi: (bi, 0, 0))]
    if rope is not None:
        qpe, kpe = rope
        args += [qpe, kpe]
        specs += [pl.BlockSpec((1, tq, LANES), lambda bi, j, qi: (bi, qi, j // 2)),
                  pl.BlockSpec((1, s, LANES), lambda bi, j, qi: (bi, 0, 0))]
    return pl.pallas_call(
        functools.partial(_flash_kernel, tq=tq, tk=tk, scale=scale,
                          has_decay=decay is not None, has_rope=rope is not None),
        out_shape=jax.ShapeDtypeStruct((b, s, MIX_WIDTH), BF16),
        grid=(b, n_pairs, s // tq),
        in_specs=specs,
        out_specs=qspec,
        compiler_params=_cparams(("arbitrary", "arbitrary", "arbitrary")),
        name="flash",
    )(*args)


def _mix_kernel(main_ref, mq_ref, mk_ref, mv_ref, wo_ref, x_ref, g_ref, b_ref, o_ref, *, alpha):
    lane = lax.broadcasted_iota(jnp.int32, (1, LANES), 1)
    first = (lane // HEAD_DIM) == 0
    mix = jnp.dot(main_ref[0], wo_ref[:MIX_WIDTH, :], preferred_element_type=F32)
    for pr in range(MEM_WIDTH // LANES):
        sl = slice(pr * LANES, (pr + 1) * LANES)
        q2 = mq_ref[0, :, sl]
        k2 = mk_ref[0, :, sl]
        v2 = mv_ref[0, :, sl]
        outs = []
        for hh in range(2):
            qm = jnp.where((lane // HEAD_DIM) == hh, q2, jnp.zeros_like(q2))
            s = lax.dot_general(qm, k2, (((1,), (1,)), ((), ())), preferred_element_type=F32)
            m = jnp.max(s, axis=-1, keepdims=True)
            p = jnp.exp(s - m)
            l = jnp.sum(p, axis=-1, keepdims=True)
            outs.append(jnp.dot(p.astype(BF16), v2, preferred_element_type=F32) / l)
        mo = jnp.where(first, outs[0], outs[1]).astype(BF16)
        mix = mix + jnp.dot(mo, wo_ref[MIX_WIDTH + pr * LANES:MIX_WIDTH + (pr + 1) * LANES, :],
                            preferred_element_type=F32)
    y = alpha * x_ref[0] + mix
    o_ref[0] = _layer_norm(y, g_ref[...], b_ref[...])


def _mix_out(main, mq, memkv, layer, wo, x, g, b, alpha):
    bsz, s, d = x.shape
    tm = PROJ_TM
    m = memkv.shape[1]
    tok = lambda w_: pl.BlockSpec((1, tm, w_), lambda bi, si: (bi, si, 0))
    return pl.pallas_call(
        functools.partial(_mix_kernel, alpha=alpha),
        out_shape=jax.ShapeDtypeStruct((bsz, s, d), F32),
        grid=(bsz, s // tm),
        in_specs=[tok(MIX_WIDTH), tok(MEM_WIDTH),
                  pl.BlockSpec((1, m, MEM_WIDTH), lambda bi, si: (bi, 0, 2 * layer)),
                  pl.BlockSpec((1, m, MEM_WIDTH), lambda bi, si: (bi, 0, 2 * layer + 1)),
                  _const_spec(wo.shape), tok(d), _const_spec((1, d)), _const_spec((1, d))],
        out_specs=tok(d),
        compiler_params=_cparams(("arbitrary", "arbitrary")),
        name="mix_out",
    )(main, mq, memkv, memkv, wo, x, g, b)


def _prep_ffn(wg, wu, wd):
    d, f = wg.shape
    nc = f // FFN_TF
    wg3 = wg.reshape(d, nc, FFN_TF).transpose(1, 0, 2)
    wu3 = wu.reshape(d, nc, FFN_TF).transpose(1, 0, 2)
    wgu = jnp.concatenate([wg3, wu3], axis=-1).astype(BF16)
    return wgu, wd.reshape(nc, FFN_TF, d).astype(BF16)


def kernel(x, mem, positions, ln_g, ln_b, ffn_w_gate, ffn_w_up, ffn_w_down, mem_w_kv, w_o,
           fox_w_in, fox_b_f, pool_w_in, pool_w_grp, pool_scale, mla_w_in, mla_q_norm,
           mla_kv_norm, mla_w_uq, mla_w_ukv):
    bsz, s, d = x.shape
    depth = ln_g.shape[0]
    alpha = float((2 * depth) ** 0.25)
    att_scale = HEAD_DIM ** -0.5
    mw = MIX_WIDTH

    memkv = _mem_kv(mem, jnp.concatenate([mem_w_kv[i] for i in range(depth)], axis=1).astype(BF16))

    inv_freq = ROPE_THETA ** (-jnp.arange(0, QK_ROPE_DIM, 2, dtype=F32) / QK_ROPE_DIM)
    ang = positions.astype(F32)[..., None] * inv_freq
    cos = jnp.cos(ang)
    sin = jnp.sin(ang)
    reps = LANES // QK_ROPE_DIM
    cos_t = jnp.tile(jnp.concatenate([cos, cos], axis=-1), (1, 1, reps))
    sin_t = jnp.tile(jnp.concatenate([-sin, sin], axis=-1), (1, 1, reps))

    def ffn(xc, i, half):
        wgu, wd = _prep_ffn(ffn_w_gate[i, half], ffn_w_up[i, half], ffn_w_down[i, half])
        k = 0 if half == 0 else 2
        y = _ffn_ln(xc.reshape(bsz * s, d), wgu, wd, ln_g[i, k][None], ln_b[i, k][None], alpha)
        return y.reshape(bsz, s, d)

    for i in range(depth):
        kind, jj = i % N_MIXERS, i // N_MIXERS
        x = ffn(x, i, 0)
        if kind == 0:
            w = fox_w_in[jj]
            w = jnp.concatenate([
                w[:, :mw] * att_scale, w[:, mw:3 * mw],
                jnp.pad(w[:, 3 * mw:3 * mw + N_MIX_HEADS], ((0, 0), (0, LANES - N_MIX_HEADS))),
                w[:, 3 * mw + N_MIX_HEADS:] * att_scale], axis=1).astype(BF16)
            bf_pad = jnp.pad(fox_b_f[jj], (0, LANES - N_MIX_HEADS))[None]
            q, k, v, dcol, drow, mq = _fox_proj(x, w, bf_pad)
            main = _flash(q, k, v, scale=1.0, decay=(dcol, drow))
        elif kind == 1:
            w = pool_w_in[jj]
            w = jnp.concatenate([w[:, :mw], w[:, mw:] * att_scale], axis=1).astype(BF16)
            w_bd = jax.scipy.linalg.block_diag(*[pool_w_grp[jj, gidx]
                                                 for gidx in range(len(POOL_WINDOWS))]).astype(BF16)
            main, mq = _pool_proj(x, w, w_bd, pool_scale[jj][None])
        else:
            w = mla_w_in[jj]
            o2 = Q_LORA_RANK + KV_LORA_RANK
            w = jnp.concatenate([w[:, :o2], jnp.tile(w[:, o2:o2 + QK_ROPE_DIM], (1, reps)),
                                 w[:, o2 + QK_ROPE_DIM:] * att_scale], axis=1).astype(BF16)
            qd = QK_NOPE_DIM + QK_ROPE_DIM
            wuq = mla_w_uq[jj].reshape(Q_LORA_RANK, N_MIX_HEADS, qd)
            wuq = jnp.concatenate([wuq[:, :, :QK_NOPE_DIM].reshape(Q_LORA_RANK, -1),
                                   wuq[:, :, QK_NOPE_DIM:].reshape(Q_LORA_RANK, -1)],
                                  axis=1).astype(BF16)
            kvd = QK_NOPE_DIM + V_HEAD_DIM
            wukv = mla_w_ukv[jj].reshape(KV_LORA_RANK, N_MIX_HEADS, kvd)
            wukv = jnp.concatenate([wukv[:, :, :QK_NOPE_DIM].reshape(KV_LORA_RANK, -1),
                                    wukv[:, :, QK_NOPE_DIM:].reshape(KV_LORA_RANK, -1)],
                                   axis=1).astype(BF16)
            qn, qpe, kn, kpe, v, mq = _mla_proj(x, w, mla_q_norm[jj][None], mla_kv_norm[jj][None],
                                                wuq, wukv, cos_t, sin_t)
            main = _flash(qn, kn, v, scale=float(qd ** -0.5), rope=(qpe, kpe))
        x = _mix_out(main, mq, memkv, i, w_o[i].astype(BF16), x, ln_g[i, 1][None],
                     ln_b[i, 1][None], alpha)
        x = ffn(x, i, 1)
    return x
```

```python
import functools

import numpy as np
import jax
import jax.numpy as jnp
from jax import lax
from jax.experimental import pallas as pl
from jax.experimental.pallas import tpu as pltpu

F32 = jnp.float32
BF16 = jnp.bfloat16

N_MIXERS = 3
HEAD_DIM = 64
MIX_WIDTH = 768
N_MIX_HEADS = MIX_WIDTH // HEAD_DIM
N_MEM_HEADS = 4
MEM_WIDTH = N_MEM_HEADS * HEAD_DIM
POOL_WINDOWS = (2, 4, 8, 16)
POOL_GROUP_DIM = MIX_WIDTH // len(POOL_WINDOWS)
Q_LORA_RANK = 384
KV_LORA_RANK = 256
QK_NOPE_DIM = 64
QK_ROPE_DIM = 32
V_HEAD_DIM = 64
ROPE_THETA = 10000.0
LN_EPS = 1e-5
RMS_EPS = 1e-6

LANES = 128
VMEM_LIMIT_BYTES = 56 * 1024 * 1024

FFN_TM = 512
FFN_TF = 256
PROJ_TM = 512
ATT_T = 256
ATT_PAIRS = 2
N_PAIRS = N_MIX_HEADS // 2
NEG_BIG = -1e30
N_SPLIT = 3


def _cparams(sem):
    return pltpu.CompilerParams(dimension_semantics=sem, vmem_limit_bytes=VMEM_LIMIT_BYTES)


def _const_spec(shape):
    nd = len(shape)
    return pl.BlockSpec(shape, lambda *_: (0,) * nd, pipeline_mode=pl.Buffered(1))


def _layer_norm(y, g, b):
    mu = jnp.mean(y, axis=-1, keepdims=True)
    d = y - mu
    var = jnp.mean(d * d, axis=-1, keepdims=True)
    return d * lax.rsqrt(var + LN_EPS) * g + b


def _rms_norm(y, g):
    ms = jnp.mean(y * y, axis=-1, keepdims=True)
    return y * lax.rsqrt(ms + RMS_EPS) * g


def _dot_nt(a, b):
    return lax.dot_general(a, b, (((1,), (1,)), ((), ())), preferred_element_type=F32)


def _ffn_kernel(x_ref, wg_ref, wu_ref, wd_ref, g_ref, b_ref, o_ref, acc_ref, *, alpha, n_chunks, tf):
    x = x_ref[...]
    xb = x.astype(BF16)

    def chunk(c):
        lo = pl.multiple_of(c * tf, tf)
        gate = jnp.dot(xb, wg_ref[:, pl.ds(lo, tf)], preferred_element_type=F32)
        up = jnp.dot(xb, wu_ref[:, pl.ds(lo, tf)], preferred_element_type=F32)
        h = (gate * jax.nn.sigmoid(gate)) * up
        return jnp.dot(h.astype(BF16), wd_ref[pl.ds(lo, tf), :], preferred_element_type=F32)

    acc_ref[...] = chunk(0)

    def body(c, carry):
        acc_ref[...] += chunk(c)
        return carry

    lax.fori_loop(1, n_chunks, body, 0)
    y = alpha * x + 0.5 * acc_ref[...]
    o_ref[...] = _layer_norm(y, g_ref[...], b_ref[...])


def _ffn_ln(x2d, wg, wu, wd, g, b, alpha):
    t, d = x2d.shape
    f = wg.shape[1]
    tm, tf = FFN_TM, FFN_TF
    return pl.pallas_call(
        functools.partial(_ffn_kernel, alpha=alpha, n_chunks=f // tf, tf=tf),
        out_shape=jax.ShapeDtypeStruct((t, d), F32),
        grid=(t // tm,),
        in_specs=[
            pl.BlockSpec((tm, d), lambda i: (i, 0)),
            _const_spec(wg.shape),
            _const_spec(wu.shape),
            _const_spec(wd.shape),
            _const_spec((1, d)),
            _const_spec((1, d)),
        ],
        out_specs=pl.BlockSpec((tm, d), lambda i: (i, 0)),
        scratch_shapes=[pltpu.VMEM((tm, d), F32)],
        compiler_params=_cparams(("arbitrary",)),
        name="ffn_ln",
    )(x2d, wg, wu, wd, g, b)


def _memkv_kernel(m_ref, w_ref, o_ref):
    o_ref[0] = jnp.dot(m_ref[0].astype(BF16), w_ref[...], preferred_element_type=F32).astype(BF16)


def _mem_kv(mem, w_all):
    b, m, d = mem.shape
    n = w_all.shape[1]
    return pl.pallas_call(
        _memkv_kernel,
        out_shape=jax.ShapeDtypeStruct((b, m, n), BF16),
        grid=(b,),
        in_specs=[pl.BlockSpec((1, m, d), lambda i: (i, 0, 0)), _const_spec(w_all.shape)],
        out_specs=pl.BlockSpec((1, m, n), lambda i: (i, 0, 0)),
        compiler_params=_cparams(("arbitrary",)),
        name="mem_kv",
    )(mem, w_all)


def _decay_placement():
    pq = np.zeros((N_SPLIT * LANES, MIX_WIDTH), np.float32)
    pk = np.zeros((N_SPLIT * LANES, MIX_WIDTH), np.float32)
    cq = np.zeros((1, MIX_WIDTH), np.float32)
    ck = np.zeros((1, MIX_WIDTH), np.float32)
    for j in range(N_PAIRS):
        base = j * LANES
        ck[0, base:base + 2 * N_SPLIT] = 1.0
        cq[0, base + 2 * N_SPLIT:base + 4 * N_SPLIT] = -1.0
        for hh in range(2):
            for c in range(N_SPLIT):
                pq[c * LANES + 2 * j + hh, base + hh * N_SPLIT + c] = 1.0
                pk[c * LANES + 2 * j + hh, base + 2 * N_SPLIT + hh * N_SPLIT + c] = 1.0
    return pq, pk, cq, ck


def _fox_ext_lanes(hh):
    return tuple(range(hh * N_SPLIT, (hh + 1) * N_SPLIT)) + tuple(
        range(2 * N_SPLIT + hh * N_SPLIT, 2 * N_SPLIT + (hh + 1) * N_SPLIT))


def _fox_proj_kernel(x_ref, w_ref, wvt_ref, bf_ref, pq_ref, pk_ref, cq_ref, ck_ref,
                     q_ref, k_ref, vt_ref, qx_ref, kx_ref, mq_ref, carry_ref, *, tm):
    si = pl.program_id(1)
    xb = x_ref[0].astype(BF16)
    h = jnp.dot(xb, w_ref[...], preferred_element_type=F32)
    mw = MIX_WIDTH
    q_ref[0] = h[:, :mw].astype(BF16)
    k_ref[0] = h[:, mw:2 * mw].astype(BF16)
    mq_ref[0] = h[:, 2 * mw + LANES:].astype(BF16)
    vt_ref[0] = _dot_nt(wvt_ref[...], xb).astype(BF16)

    lf = jax.nn.log_sigmoid(h[:, 2 * mw:2 * mw + LANES] + bf_ref[...])
    row = lax.broadcasted_iota(jnp.int32, (tm, LANES), 0)
    shift = 1
    while shift < tm:
        lf = lf + jnp.where(row >= shift, pltpu.roll(lf, shift, 0), 0.0)
        shift *= 2

    @pl.when(si == 0)
    def _():
        carry_ref[...] = jnp.zeros_like(carry_ref)

    cum = lf + carry_ref[0:1, :]
    carry_ref[...] = jnp.broadcast_to(cum[tm - 1:tm, :], carry_ref.shape)

    parts = []
    rem = cum
    for _ in range(N_SPLIT):
        part = rem.astype(BF16)
        parts.append(part)
        rem = rem - part.astype(F32)
    split = jnp.concatenate(parts, axis=1)
    qx_ref[0] = (jnp.dot(split, pq_ref[...], preferred_element_type=F32)
                 + cq_ref[...]).astype(BF16)
    kx_ref[0] = (jnp.dot(split, pk_ref[...], preferred_element_type=F32)
                 + ck_ref[...]).astype(BF16)


def _fox_proj(x, w, wvt, bf_pad):
    b, s, d = x.shape
    tm = PROJ_TM
    pq, pk, cq, ck = _decay_placement()
    pq = jnp.asarray(pq, BF16)
    pk = jnp.asarray(pk, BF16)
    cq = jnp.asarray(cq)
    ck = jnp.asarray(ck)
    tok = lambda w_: pl.BlockSpec((1, tm, w_), lambda bi, si: (bi, si, 0))
    sd = lambda w_: jax.ShapeDtypeStruct((b, s, w_), BF16)
    return pl.pallas_call(
        functools.partial(_fox_proj_kernel, tm=tm),
        out_shape=(sd(MIX_WIDTH), sd(MIX_WIDTH), jax.ShapeDtypeStruct((b, MIX_WIDTH, s), BF16),
                   sd(MIX_WIDTH), sd(MIX_WIDTH), sd(MEM_WIDTH)),
        grid=(b, s // tm),
        in_specs=[tok(d), _const_spec(w.shape), _const_spec(wvt.shape), _const_spec((1, LANES)),
                  _const_spec(pq.shape), _const_spec(pk.shape), _const_spec(cq.shape),
                  _const_spec(ck.shape)],
        out_specs=(tok(MIX_WIDTH), tok(MIX_WIDTH),
                   pl.BlockSpec((1, MIX_WIDTH, tm), lambda bi, si: (bi, 0, si)),
                   tok(MIX_WIDTH), tok(MIX_WIDTH), tok(MEM_WIDTH)),
        scratch_shapes=[pltpu.VMEM((8, LANES), F32)],
        compiler_params=_cparams(("arbitrary", "arbitrary")),
        name="fox_proj",
    )(x, w, wvt, bf_pad, pq, pk, cq, ck)


def _pool_kernel(x_ref, w_ref, wg_ref, sc_ref, main_ref, mq_ref, carry_ref, *, tm):
    si = pl.program_id(1)
    halo = carry_ref.shape[0]
    xb = x_ref[0].astype(BF16)
    h = jnp.dot(xb, w_ref[...], preferred_element_type=F32)
    u = h[:, :MIX_WIDTH]
    mq_ref[0] = h[:, MIX_WIDTH:].astype(BF16)

    @pl.when(si == 0)
    def _():
        carry_ref[...] = jnp.zeros_like(carry_ref)

    ext = jnp.concatenate([carry_ref[...], u], axis=0)
    carry_ref[...] = u[tm - halo:, :]

    lane = lax.broadcasted_iota(jnp.int32, (1, MIX_WIDTH), 1)
    t_pos = si * tm + lax.broadcasted_iota(jnp.int32, (tm, 1), 0)
    acc = ext
    pooled = None
    span = 1
    for gi, win in enumerate(POOL_WINDOWS):
        while span < win:
            acc = acc + pltpu.roll(acc, span, 0)
            span *= 2
        cnt = jnp.minimum(t_pos + 1, win).astype(F32)
        mean = acc[halo:, :] / cnt
        if pooled is None:
            pooled = mean
        else:
            pooled = jnp.where(lane >= gi * POOL_GROUP_DIM, mean, pooled)
    pooled = pooled - u
    y = jnp.dot(pooled.astype(BF16), wg_ref[...], preferred_element_type=F32)
    main_ref[0] = (y * sc_ref[...]).astype(BF16)


def _pool_proj(x, w, w_bd, scale):
    b, s, d = x.shape
    tm = PROJ_TM
    halo = 16
    tok = lambda w_: pl.BlockSpec((1, tm, w_), lambda bi, si: (bi, si, 0))
    return pl.pallas_call(
        functools.partial(_pool_kernel, tm=tm),
        out_shape=(jax.ShapeDtypeStruct((b, s, MIX_WIDTH), BF16),
                   jax.ShapeDtypeStruct((b, s, MEM_WIDTH), BF16)),
        grid=(b, s // tm),
        in_specs=[tok(d), _const_spec(w.shape), _const_spec(w_bd.shape),
                  _const_spec((1, MIX_WIDTH))],
        out_specs=(tok(MIX_WIDTH), tok(MEM_WIDTH)),
        scratch_shapes=[pltpu.VMEM((halo, MIX_WIDTH), F32)],
        compiler_params=_cparams(("arbitrary", "arbitrary")),
        name="pool_proj",
    )(x, w, w_bd, scale)


def _rope(xv, cos_t, sin_t, lane):
    half = QK_ROPE_DIM // 2
    swapped = jnp.where((lane % QK_ROPE_DIM) < half,
                        pltpu.roll(xv, LANES - half, 1), pltpu.roll(xv, half, 1))
    return xv * cos_t + swapped * sin_t


def _mla_proj_kernel(x_ref, w_ref, qn_ref, kvn_ref, wuq_ref, wuk_ref, wuvt_ref, cos_ref, sin_ref,
                     qnope_ref, qpe_ref, knope_ref, kpe_ref, vt_ref, mq_ref, *, scale):
    xb = x_ref[0].astype(BF16)
    h = jnp.dot(xb, w_ref[...], preferred_element_type=F32)
    o1 = Q_LORA_RANK
    o2 = o1 + KV_LORA_RANK
    o3 = o2 + LANES
    cq = _rms_norm(h[:, :o1], qn_ref[...]).astype(BF16)
    ckv = _rms_norm(h[:, o1:o2], kvn_ref[...]).astype(BF16)
    mq_ref[0] = h[:, o3:].astype(BF16)
    cos_t = cos_ref[0]
    sin_t = sin_ref[0]
    lane = lax.broadcasted_iota(jnp.int32, (1, LANES), 1)
    kpe_ref[0] = _rope(h[:, o2:o3], cos_t, sin_t, lane).astype(BF16)

    q = jnp.dot(cq, wuq_ref[...], preferred_element_type=F32) * scale
    qnope_ref[0] = q[:, :MIX_WIDTH].astype(BF16)
    for blk in range(N_PAIRS):
        lo = MIX_WIDTH + blk * LANES
        qpe_ref[0, :, blk * LANES:(blk + 1) * LANES] = _rope(
            q[:, lo:lo + LANES], cos_t, sin_t, lane).astype(BF16)
    knope_ref[0] = jnp.dot(ckv, wuk_ref[...], preferred_element_type=F32).astype(BF16)
    vt_ref[0] = _dot_nt(wuvt_ref[...], ckv).astype(BF16)


def _mla_proj(x, w, qn, kvn, wuq, wuk, wuvt, cos_t, sin_t, scale):
    b, s, d = x.shape
    tm = PROJ_TM
    tok = lambda w_: pl.BlockSpec((1, tm, w_), lambda bi, si: (bi, si, 0))
    sd = lambda w_: jax.ShapeDtypeStruct((b, s, w_), BF16)
    return pl.pallas_call(
        functools.partial(_mla_proj_kernel, scale=scale),
        out_shape=(sd(MIX_WIDTH), sd(MIX_WIDTH), sd(MIX_WIDTH), sd(LANES),
                   jax.ShapeDtypeStruct((b, MIX_WIDTH, s), BF16), sd(MEM_WIDTH)),
        grid=(b, s // tm),
        in_specs=[tok(d), _const_spec(w.shape), _const_spec(qn.shape), _const_spec(kvn.shape),
                  _const_spec(wuq.shape), _const_spec(wuk.shape), _const_spec(wuvt.shape),
                  tok(LANES), tok(LANES)],
        out_specs=(tok(MIX_WIDTH), tok(MIX_WIDTH), tok(MIX_WIDTH), tok(LANES),
                   pl.BlockSpec((1, MIX_WIDTH, tm), lambda bi, si: (bi, 0, si)), tok(MEM_WIDTH)),
        compiler_params=_cparams(("arbitrary", "arbitrary")),
        name="mla_proj",
    )(x, w, qn, kvn, wuq, wuk, wuvt, cos_t, sin_t)


def _flash_kernel(qa_ref, qb_ref, ka_ref, kb_ref, vt_ref, o_ref, m_ref, l_ref, acc_ref, *,
                  t, pairs, kb_shared, b_lanes):
    qi = pl.program_id(2)
    lane = lax.broadcasted_iota(jnp.int32, (1, LANES), 1)
    n_heads = 2 * pairs

    qx = []
    for p in range(pairs):
        qa = qa_ref[0, :, p * LANES:(p + 1) * LANES]
        qb = qb_ref[0, :, p * LANES:(p + 1) * LANES]
        for hh in range(2):
            bmask = functools.reduce(jnp.logical_or, [lane == ln for ln in b_lanes[hh]]) \
                if len(b_lanes[hh]) <= 8 else ((lane >= b_lanes[hh][0]) & (lane <= b_lanes[hh][-1]))
            qx.append(jnp.concatenate(
                [jnp.where((lane // HEAD_DIM) == hh, qa, jnp.zeros_like(qa)),
                 jnp.where(bmask, qb, jnp.zeros_like(qb))], axis=1))

    for idx in range(n_heads):
        m_ref[idx] = jnp.full((1, t), NEG_BIG, F32)
        l_ref[idx] = jnp.zeros((1, t), F32)
        acc_ref[idx] = jnp.zeros((HEAD_DIM, t), F32)

    def step(kb, masked):
        ks = pl.multiple_of(kb * t, t)
        for p in range(pairs):
            kb_lo = 0 if kb_shared else p * LANES
            kx = jnp.concatenate([ka_ref[0, pl.ds(ks, t), p * LANES:(p + 1) * LANES],
                                  kb_ref[0, pl.ds(ks, t), kb_lo:kb_lo + LANES]], axis=1)
            for hh in range(2):
                idx = 2 * p + hh
                st = _dot_nt(kx, qx[idx])
                if masked:
                    kr = lax.broadcasted_iota(jnp.int32, (t, t), 0)
                    qc = lax.broadcasted_iota(jnp.int32, (t, t), 1)
                    st = jnp.where(kr <= qc, st, -jnp.inf)
                m_old = m_ref[idx]
                m_new = jnp.maximum(m_old, jnp.max(st, axis=0, keepdims=True))
                a = jnp.exp(m_old - m_new)
                pt = jnp.exp(st - m_new)
                l_ref[idx] = a * l_ref[idx] + jnp.sum(pt, axis=0, keepdims=True)
                m_ref[idx] = m_new
                vt = vt_ref[0, p * LANES + hh * HEAD_DIM:p * LANES + (hh + 1) * HEAD_DIM,
                            pl.ds(ks, t)]
                acc_ref[idx] = a * acc_ref[idx] + jnp.dot(vt, pt.astype(BF16),
                                                          preferred_element_type=F32)

    def body(kb, carry):
        step(kb, False)
        return carry

    lax.fori_loop(0, qi, body, 0)
    step(qi, True)

    for p in range(pairs):
        ot = jnp.concatenate([acc_ref[2 * p] * (1.0 / l_ref[2 * p]),
                              acc_ref[2 * p + 1] * (1.0 / l_ref[2 * p + 1])], axis=0)
        o_ref[0, :, p * LANES:(p + 1) * LANES] = ot.T.astype(BF16)


def _flash(qa, qb, ka, kb, vt, *, kb_shared, b_lanes):
    b, s, _ = qa.shape
    t = ATT_T
    pairs = ATT_PAIRS
    gw = pairs * LANES
    qspec = pl.BlockSpec((1, t, gw), lambda bi, g, qi: (bi, qi, g))
    kspec = pl.BlockSpec((1, s, gw), lambda bi, g, qi: (bi, 0, g))
    kbspec = pl.BlockSpec((1, s, LANES), lambda bi, g, qi: (bi, 0, 0)) if kb_shared else kspec
    vspec = pl.BlockSpec((1, gw, s), lambda bi, g, qi: (bi, g, 0))
    nh = 2 * pairs
    return pl.pallas_call(
        functools.partial(_flash_kernel, t=t, pairs=pairs, kb_shared=kb_shared, b_lanes=b_lanes),
        out_shape=jax.ShapeDtypeStruct((b, s, MIX_WIDTH), BF16),
        grid=(b, N_PAIRS // pairs, s // t),
        in_specs=[qspec, qspec, kspec, kbspec, vspec],
        out_specs=qspec,
        scratch_shapes=[pltpu.VMEM((nh, 1, t), F32), pltpu.VMEM((nh, 1, t), F32),
                        pltpu.VMEM((nh, HEAD_DIM, t), F32)],
        compiler_params=_cparams(("arbitrary", "arbitrary", "arbitrary")),
        name="flash",
    )(qa, qb, ka, kb, vt)


def _mix_kernel(main_ref, mq_ref, mk_ref, mv_ref, wo_ref, x_ref, g_ref, b_ref, o_ref, *, alpha):
    lane = lax.broadcasted_iota(jnp.int32, (1, LANES), 1)
    first = (lane // HEAD_DIM) == 0
    mix = jnp.dot(main_ref[0], wo_ref[:MIX_WIDTH, :], preferred_element_type=F32)
    for pr in range(MEM_WIDTH // LANES):
        sl = slice(pr * LANES, (pr + 1) * LANES)
        q2 = mq_ref[0, :, sl]
        k2 = mk_ref[0, :, sl]
        v2 = mv_ref[0, :, sl]
        outs = []
        for hh in range(2):
            qm = jnp.where((lane // HEAD_DIM) == hh, q2, jnp.zeros_like(q2))
            s = _dot_nt(qm, k2)
            m = jnp.max(s, axis=-1, keepdims=True)
            p = jnp.exp(s - m)
            l = jnp.sum(p, axis=-1, keepdims=True)
            outs.append(jnp.dot(p.astype(BF16), v2, preferred_element_type=F32) / l)
        mo = jnp.where(first, outs[0], outs[1]).astype(BF16)
        mix = mix + jnp.dot(mo, wo_ref[MIX_WIDTH + pr * LANES:MIX_WIDTH + (pr + 1) * LANES, :],
                            preferred_element_type=F32)
    y = alpha * x_ref[0] + mix
    o_ref[0] = _layer_norm(y, g_ref[...], b_ref[...])


def _mix_out(main, mq, memkv, layer, wo, x, g, b, alpha):
    bsz, s, d = x.shape
    tm = PROJ_TM
    m = memkv.shape[1]
    tok = lambda w_: pl.BlockSpec((1, tm, w_), lambda bi, si: (bi, si, 0))
    return pl.pallas_call(
        functools.partial(_mix_kernel, alpha=alpha),
        out_shape=jax.ShapeDtypeStruct((bsz, s, d), F32),
        grid=(bsz, s // tm),
        in_specs=[tok(MIX_WIDTH), tok(MEM_WIDTH),
                  pl.BlockSpec((1, m, MEM_WIDTH), lambda bi, si: (bi, 0, 2 * layer)),
                  pl.BlockSpec((1, m, MEM_WIDTH), lambda bi, si: (bi, 0, 2 * layer + 1)),
                  _const_spec(wo.shape), tok(d), _const_spec((1, d)), _const_spec((1, d))],
        out_specs=tok(d),
        compiler_params=_cparams(("arbitrary", "arbitrary")),
        name="mix_out",
    )(main, mq, memkv, memkv, wo, x, g, b)


def _pair_pad_cols(w3, lo, width):
    k = w3.shape[0]
    part = w3[:, :, lo:lo + width].reshape(k, N_PAIRS, 2 * width)
    return jnp.pad(part, ((0, 0), (0, 0), (0, LANES - 2 * width))).reshape(k, N_PAIRS * LANES)


def kernel(x, mem, positions, ln_g, ln_b, ffn_w_gate, ffn_w_up, ffn_w_down, mem_w_kv, w_o,
           fox_w_in, fox_b_f, pool_w_in, pool_w_grp, pool_scale, mla_w_in, mla_q_norm,
           mla_kv_norm, mla_w_uq, mla_w_ukv):
    bsz, s, d = x.shape
    depth = ln_g.shape[0]
    alpha = float((2 * depth) ** 0.25)
    att_scale = HEAD_DIM ** -0.5
    mw = MIX_WIDTH

    memkv = _mem_kv(mem, jnp.concatenate([mem_w_kv[i] for i in range(depth)], axis=1).astype(BF16))

    inv_freq = ROPE_THETA ** (-jnp.arange(0, QK_ROPE_DIM, 2, dtype=F32) / QK_ROPE_DIM)
    ang = positions.astype(F32)[..., None] * inv_freq
    cos = jnp.cos(ang)
    sin = jnp.sin(ang)
    zpad = ((0, 0), (0, 0), (0, LANES - 2 * QK_ROPE_DIM))
    cos_t = jnp.pad(jnp.concatenate([cos, cos, cos, cos], axis=-1), zpad)
    sin_t = jnp.pad(jnp.concatenate([-sin, sin, -sin, sin], axis=-1), zpad)

    def ffn(xc, i, half):
        k = 0 if half == 0 else 2
        y = _ffn_ln(xc.reshape(bsz * s, d), ffn_w_gate[i, half].astype(BF16),
                    ffn_w_up[i, half].astype(BF16), ffn_w_down[i, half].astype(BF16),
                    ln_g[i, k][None], ln_b[i, k][None], alpha)
        return y.reshape(bsz, s, d)

    for i in range(depth):
        kind, jj = i % N_MIXERS, i // N_MIXERS
        x = ffn(x, i, 0)
        if kind == 0:
            w = fox_w_in[jj]
            w_main = jnp.concatenate([
                w[:, :mw] * att_scale, w[:, mw:2 * mw],
                jnp.pad(w[:, 3 * mw:3 * mw + N_MIX_HEADS], ((0, 0), (0, LANES - N_MIX_HEADS))),
                w[:, 3 * mw + N_MIX_HEADS:] * att_scale], axis=1).astype(BF16)
            wvt = w[:, 2 * mw:3 * mw].T.astype(BF16)
            bf_pad = jnp.pad(fox_b_f[jj], (0, LANES - N_MIX_HEADS))[None]
            q, k, vt, qx, kx, mq = _fox_proj(x, w_main, wvt, bf_pad)
            main = _flash(q, qx, k, kx, vt, kb_shared=False,
                          b_lanes=(_fox_ext_lanes(0), _fox_ext_lanes(1)))
        elif kind == 1:
            w = pool_w_in[jj]
            w = jnp.concatenate([w[:, :mw], w[:, mw:] * att_scale], axis=1).astype(BF16)
            w_bd = jax.scipy.linalg.block_diag(*[pool_w_grp[jj, gidx]
                                                 for gidx in range(len(POOL_WINDOWS))]).astype(BF16)
            main, mq = _pool_proj(x, w, w_bd, pool_scale[jj][None])
        else:
            w = mla_w_in[jj]
            o2 = Q_LORA_RANK + KV_LORA_RANK
            kr = w[:, o2:o2 + QK_ROPE_DIM]
            w = jnp.concatenate([w[:, :o2], kr, kr,
                                 jnp.zeros((d, LANES - 2 * QK_ROPE_DIM), F32),
                                 w[:, o2 + QK_ROPE_DIM:] * att_scale], axis=1).astype(BF16)
            qd = QK_NOPE_DIM + QK_ROPE_DIM
            wuq3 = mla_w_uq[jj].reshape(Q_LORA_RANK, N_MIX_HEADS, qd)
            wuq = jnp.concatenate([wuq3[:, :, :QK_NOPE_DIM].reshape(Q_LORA_RANK, mw),
                                   _pair_pad_cols(wuq3, QK_NOPE_DIM, QK_ROPE_DIM)],
                                  axis=1).astype(BF16)
            kvd = QK_NOPE_DIM + V_HEAD_DIM
            wukv3 = mla_w_ukv[jj].reshape(KV_LORA_RANK, N_MIX_HEADS, kvd)
            wuk = wukv3[:, :, :QK_NOPE_DIM].reshape(KV_LORA_RANK, mw).astype(BF16)
            wuvt = wukv3[:, :, QK_NOPE_DIM:].reshape(KV_LORA_RANK, mw).T.astype(BF16)
            qn, qpe, kn, kpe, vt, mq = _mla_proj(
                x, w, mla_q_norm[jj][None], mla_kv_norm[jj][None], wuq, wuk, wuvt, cos_t, sin_t,
                float(qd ** -0.5))
            rope_lanes = (tuple(range(QK_ROPE_DIM)), tuple(range(QK_ROPE_DIM, 2 * QK_ROPE_DIM)))
            main = _flash(qn, qpe, kn, kpe, vt, kb_shared=True, b_lanes=rope_lanes)
        x = _mix_out(main, mq, memkv, i, w_o[i].astype(BF16), x, ln_g[i, 1][None],
                     ln_b[i, 1][None], alpha)
        x = ffn(x, i, 1)
    return x
```

```python
import functools
import math

import numpy as np
import jax
import jax.numpy as jnp
from jax import lax
from jax.experimental import pallas as pl
from jax.experimental.pallas import tpu as pltpu

F32 = jnp.float32
BF16 = jnp.bfloat16

N_MIXERS = 3
HEAD_DIM = 64
MIX_WIDTH = 768
N_MIX_HEADS = MIX_WIDTH // HEAD_DIM
N_MEM_HEADS = 4
MEM_WIDTH = N_MEM_HEADS * HEAD_DIM
POOL_WINDOWS = (2, 4, 8, 16)
POOL_GROUP_DIM = MIX_WIDTH // len(POOL_WINDOWS)
Q_LORA_RANK = 384
KV_LORA_RANK = 256
QK_NOPE_DIM = 64
QK_ROPE_DIM = 32
V_HEAD_DIM = 64
ROPE_THETA = 10000.0
LN_EPS = 1e-5
RMS_EPS = 1e-6

LANES = 128
VMEM_LIMIT_BYTES = 56 * 1024 * 1024

FFN_TM = 512
ATT_T = 256
ATT_PAIRS = 6
N_PAIRS = N_MIX_HEADS // 2
ACC_ROWS = HEAD_DIM + 16
NEG_BIG = -1e30
LOG2E = math.log2(math.e)
N_SPLIT = 3


def _cparams(sem):
    return pltpu.CompilerParams(dimension_semantics=sem, vmem_limit_bytes=VMEM_LIMIT_BYTES)


def _const_spec(shape):
    nd = len(shape)
    return pl.BlockSpec(shape, lambda *_: (0,) * nd, pipeline_mode=pl.Buffered(1))


def _layer_norm(y, g, b):
    mu = jnp.mean(y, axis=-1, keepdims=True)
    d = y - mu
    var = jnp.mean(d * d, axis=-1, keepdims=True)
    return d * lax.rsqrt(var + LN_EPS) * g + b


def _rms_norm(y, g):
    ms = jnp.mean(y * y, axis=-1, keepdims=True)
    return y * lax.rsqrt(ms + RMS_EPS) * g


def _dot_nt(a, b):
    return lax.dot_general(a, b, (((1,), (1,)), ((), ())), preferred_element_type=F32)


def _ffn_body(x, wg_ref, wu_ref, wd_ref, g_ref, b_ref, alpha):
    xb = x.astype(BF16)
    gate = jnp.dot(xb, wg_ref[...], preferred_element_type=F32)
    up = jnp.dot(xb, wu_ref[...], preferred_element_type=F32)
    h = ((gate * jax.nn.sigmoid(gate)) * up).astype(BF16)
    ff = jnp.dot(h, wd_ref[...], preferred_element_type=F32)
    return _layer_norm(alpha * x + 0.5 * ff, g_ref[...], b_ref[...])


def _layer_spec(arr, *idx):
    lead = len(idx)
    shape = (None,) * lead + tuple(arr.shape[lead:])
    tail = (0,) * (arr.ndim - lead)
    return pl.BlockSpec(shape, lambda *_: tuple(idx) + tail, pipeline_mode=pl.Buffered(1))


def _ffn_specs(wts, layer, half, ln_idx):
    wg, wu, wd, ln_g, ln_b = wts
    return ([_layer_spec(wg, layer, half), _layer_spec(wu, layer, half),
             _layer_spec(wd, layer, half), _layer_spec(ln_g, layer, ln_idx),
             _layer_spec(ln_b, layer, ln_idx)], [wg, wu, wd, ln_g, ln_b])


def _ffn_proj_call(name, proj_kernel, x, wts, layer, alpha, proj_args, proj_specs, out_shape,
                   out_specs, scratch=()):
    bsz, s, d = x.shape
    tm = FFN_TM
    tok = pl.BlockSpec((1, tm, d), lambda bi, si: (bi, si, 0))
    fspecs, fargs = _ffn_specs(wts, layer, 0, 0)
    n_in = len(proj_args)

    def fused(x_ref, wg_ref, wu_ref, wd_ref, g_ref, b_ref, *rest):
        x1 = _ffn_body(x_ref[0], wg_ref, wu_ref, wd_ref, g_ref, b_ref, alpha)
        rest[n_in][0] = x1
        proj_kernel(x1, *rest[:n_in], *rest[n_in + 1:])

    return pl.pallas_call(
        fused,
        out_shape=(jax.ShapeDtypeStruct((bsz, s, d), F32),) + tuple(out_shape),
        grid=(bsz, s // tm),
        in_specs=[tok] + fspecs + list(proj_specs),
        out_specs=(tok,) + tuple(out_specs),
        scratch_shapes=list(scratch),
        compiler_params=_cparams(("arbitrary", "arbitrary")),
        name=name,
    )(x, *fargs, *proj_args)


def _memkv_kernel(m_ref, w_ref, o_ref):
    o_ref[0] = jnp.dot(m_ref[0].astype(BF16), w_ref[...], preferred_element_type=F32).astype(BF16)


def _mem_kv(mem, w_all):
    b, m, d = mem.shape
    n = w_all.shape[1]
    return pl.pallas_call(
        _memkv_kernel,
        out_shape=jax.ShapeDtypeStruct((b, m, n), BF16),
        grid=(b,),
        in_specs=[pl.BlockSpec((1, m, d), lambda i: (i, 0, 0)), _const_spec(w_all.shape)],
        out_specs=pl.BlockSpec((1, m, n), lambda i: (i, 0, 0)),
        compiler_params=_cparams(("arbitrary",)),
        name="mem_kv",
    )(mem, w_all)


def _decay_placement():
    pq = np.zeros((N_SPLIT * LANES, MIX_WIDTH), np.float32)
    pk = np.zeros((N_SPLIT * LANES, MIX_WIDTH), np.float32)
    cq = np.zeros((1, MIX_WIDTH), np.float32)
    ck = np.zeros((1, MIX_WIDTH), np.float32)
    for j in range(N_PAIRS):
        base = j * LANES
        ck[0, base:base + 2 * N_SPLIT] = 1.0
        cq[0, base + 2 * N_SPLIT:base + 4 * N_SPLIT] = -1.0
        for hh in range(2):
            for c in range(N_SPLIT):
                pq[c * LANES + 2 * j + hh, base + hh * N_SPLIT + c] = 1.0
                pk[c * LANES + 2 * j + hh, base + 2 * N_SPLIT + hh * N_SPLIT + c] = 1.0
    return pq, pk, cq, ck


def _fox_ext_lanes(hh):
    return tuple(range(hh * N_SPLIT, (hh + 1) * N_SPLIT)) + tuple(
        range(2 * N_SPLIT + hh * N_SPLIT, 2 * N_SPLIT + (hh + 1) * N_SPLIT))


def _fox_proj_kernel(x, w_ref, wvt_ref, bf_ref, pq_ref, pk_ref, cq_ref, ck_ref,
                     q_ref, k_ref, vt_ref, qx_ref, kx_ref, mq_ref, carry_ref, *, tm):
    si = pl.program_id(1)
    xb = x.astype(BF16)
    h = jnp.dot(xb, w_ref[...], preferred_element_type=F32)
    mw = MIX_WIDTH
    q_ref[0] = h[:, :mw].astype(BF16)
    k_ref[0] = h[:, mw:2 * mw].astype(BF16)
    mq_ref[0] = h[:, 2 * mw + LANES:].astype(BF16)
    vt_ref[0] = _dot_nt(wvt_ref[...], xb).astype(BF16)

    lf = jax.nn.log_sigmoid(h[:, 2 * mw:2 * mw + LANES] + bf_ref[...])
    row = lax.broadcasted_iota(jnp.int32, (tm, LANES), 0)
    shift = 1
    while shift < tm:
        lf = lf + jnp.where(row >= shift, pltpu.roll(lf, shift, 0), 0.0)
        shift *= 2

    @pl.when(si == 0)
    def _():
        carry_ref[...] = jnp.zeros_like(carry_ref)

    cum = lf + carry_ref[0:1, :]
    carry_ref[...] = jnp.broadcast_to(cum[tm - 1:tm, :], carry_ref.shape)

    parts = []
    rem = cum * LOG2E
    for _ in range(N_SPLIT):
        part = rem.astype(BF16)
        parts.append(part)
        rem = rem - part.astype(F32)
    split = jnp.concatenate(parts, axis=1)
    qx_ref[0] = (jnp.dot(split, pq_ref[...], preferred_element_type=F32)
                 + cq_ref[...]).astype(BF16)
    kx_ref[0] = (jnp.dot(split, pk_ref[...], preferred_element_type=F32)
                 + ck_ref[...]).astype(BF16)


def _ffn_fox_proj(x, wts, layer, alpha, w, wvt, bf_pad):
    b, s, d = x.shape
    tm = FFN_TM
    pq, pk, cq, ck = _decay_placement()
    pq = jnp.asarray(pq, BF16)
    pk = jnp.asarray(pk, BF16)
    cq = jnp.asarray(cq)
    ck = jnp.asarray(ck)
    tok = lambda w_: pl.BlockSpec((1, tm, w_), lambda bi, si: (bi, si, 0))
    sd = lambda w_: jax.ShapeDtypeStruct((b, s, w_), BF16)
    args = (w, wvt, bf_pad, pq, pk, cq, ck)
    return _ffn_proj_call(
        "ffn_fox_proj", functools.partial(_fox_proj_kernel, tm=tm), x, wts, layer, alpha, args,
        [_const_spec(a.shape) for a in args],
        (sd(MIX_WIDTH), sd(MIX_WIDTH), jax.ShapeDtypeStruct((b, MIX_WIDTH, s), BF16),
         sd(MIX_WIDTH), sd(MIX_WIDTH), sd(MEM_WIDTH)),
        (tok(MIX_WIDTH), tok(MIX_WIDTH),
         pl.BlockSpec((1, MIX_WIDTH, tm), lambda bi, si: (bi, 0, si)),
         tok(MIX_WIDTH), tok(MIX_WIDTH), tok(MEM_WIDTH)),
        scratch=[pltpu.VMEM((8, LANES), F32)])


def _pool_kernel(x, w_ref, wg_ref, sc_ref, main_ref, mq_ref, carry_ref, *, tm):
    si = pl.program_id(1)
    halo = carry_ref.shape[0]
    xb = x.astype(BF16)
    h = jnp.dot(xb, w_ref[...], preferred_element_type=F32)
    u = h[:, :MIX_WIDTH]
    mq_ref[0] = h[:, MIX_WIDTH:].astype(BF16)

    @pl.when(si == 0)
    def _():
        carry_ref[...] = jnp.zeros_like(carry_ref)

    ext = jnp.concatenate([carry_ref[...], u], axis=0)
    carry_ref[...] = u[tm - halo:, :]

    lane = lax.broadcasted_iota(jnp.int32, (1, MIX_WIDTH), 1)
    t_pos = si * tm + lax.broadcasted_iota(jnp.int32, (tm, 1), 0)
    acc = ext
    pooled = None
    span = 1
    for gi, win in enumerate(POOL_WINDOWS):
        while span < win:
            acc = acc + pltpu.roll(acc, span, 0)
            span *= 2
        cnt = jnp.minimum(t_pos + 1, win).astype(F32)
        mean = acc[halo:, :] / cnt
        if pooled is None:
            pooled = mean
        else:
            pooled = jnp.where(lane >= gi * POOL_GROUP_DIM, mean, pooled)
    pooled = pooled - u
    y = jnp.dot(pooled.astype(BF16), wg_ref[...], preferred_element_type=F32)
    main_ref[0] = (y * sc_ref[...]).astype(BF16)


def _ffn_pool_proj(x, wts, layer, alpha, w, w_bd, scale):
    b, s, d = x.shape
    tm = FFN_TM
    halo = 16
    tok = lambda w_: pl.BlockSpec((1, tm, w_), lambda bi, si: (bi, si, 0))
    args = (w, w_bd, scale)
    return _ffn_proj_call(
        "ffn_pool_proj", functools.partial(_pool_kernel, tm=tm), x, wts, layer, alpha, args,
        [_const_spec(a.shape) for a in args],
        (jax.ShapeDtypeStruct((b, s, MIX_WIDTH), BF16), jax.ShapeDtypeStruct((b, s, MEM_WIDTH), BF16)),
        (tok(MIX_WIDTH), tok(MEM_WIDTH)),
        scratch=[pltpu.VMEM((halo, MIX_WIDTH), F32)])


def _rope(xv, cos_t, sin_t, lane):
    half = QK_ROPE_DIM // 2
    swapped = jnp.where((lane % QK_ROPE_DIM) < half,
                        pltpu.roll(xv, LANES - half, 1), pltpu.roll(xv, half, 1))
    return xv * cos_t + swapped * sin_t


def _mla_proj_kernel(x, w_ref, qn_ref, kvn_ref, wuq_ref, wuk_ref, wuvt_ref, cos_ref, sin_ref,
                     qnope_ref, qpe_ref, knope_ref, kpe_ref, vt_ref, mq_ref, *, scale):
    xb = x.astype(BF16)
    h = jnp.dot(xb, w_ref[...], preferred_element_type=F32)
    o1 = Q_LORA_RANK
    o2 = o1 + KV_LORA_RANK
    o3 = o2 + LANES
    cq = _rms_norm(h[:, :o1], qn_ref[...]).astype(BF16)
    ckv = _rms_norm(h[:, o1:o2], kvn_ref[...]).astype(BF16)
    mq_ref[0] = h[:, o3:].astype(BF16)
    cos_t = cos_ref[0]
    sin_t = sin_ref[0]
    lane = lax.broadcasted_iota(jnp.int32, (1, LANES), 1)
    kpe_ref[0] = _rope(h[:, o2:o3], cos_t, sin_t, lane).astype(BF16)

    q = jnp.dot(cq, wuq_ref[...], preferred_element_type=F32) * scale
    qnope_ref[0] = q[:, :MIX_WIDTH].astype(BF16)
    for blk in range(N_PAIRS):
        lo = MIX_WIDTH + blk * LANES
        qpe_ref[0, :, blk * LANES:(blk + 1) * LANES] = _rope(
            q[:, lo:lo + LANES], cos_t, sin_t, lane).astype(BF16)
    knope_ref[0] = jnp.dot(ckv, wuk_ref[...], preferred_element_type=F32).astype(BF16)
    vt_ref[0] = _dot_nt(wuvt_ref[...], ckv).astype(BF16)


def _ffn_mla_proj(x, wts, layer, alpha, w, qn, kvn, wuq, wuk, wuvt, cos_t, sin_t, scale):
    b, s, d = x.shape
    tm = FFN_TM
    tok = lambda w_: pl.BlockSpec((1, tm, w_), lambda bi, si: (bi, si, 0))
    sd = lambda w_: jax.ShapeDtypeStruct((b, s, w_), BF16)
    consts = (w, qn, kvn, wuq, wuk, wuvt)
    return _ffn_proj_call(
        "ffn_mla_proj", functools.partial(_mla_proj_kernel, scale=scale), x, wts, layer, alpha,
        consts + (cos_t, sin_t),
        [_const_spec(a.shape) for a in consts] + [tok(LANES), tok(LANES)],
        (sd(MIX_WIDTH), sd(MIX_WIDTH), sd(MIX_WIDTH), sd(LANES),
         jax.ShapeDtypeStruct((b, MIX_WIDTH, s), BF16), sd(MEM_WIDTH)),
        (tok(MIX_WIDTH), tok(MIX_WIDTH), tok(MIX_WIDTH), tok(LANES),
         pl.BlockSpec((1, MIX_WIDTH, tm), lambda bi, si: (bi, 0, si)), tok(MEM_WIDTH)))


def _flash_kernel(qa_ref, qb_ref, ka_ref, kb_ref, vt_ref, o_ref, se_ref, so_ref, m_ref, acc_ref, *,
                  t, pairs, kb_shared, b_lanes):
    qi = pl.program_id(2)
    lane = lax.broadcasted_iota(jnp.int32, (1, LANES), 1)
    n_heads = 2 * pairs

    qx = []
    for p in range(pairs):
        qa = qa_ref[0, :, p * LANES:(p + 1) * LANES]
        qb = qb_ref[0, :, p * LANES:(p + 1) * LANES]
        for hh in range(2):
            bmask = functools.reduce(jnp.logical_or, [lane == ln for ln in b_lanes[hh]]) \
                if len(b_lanes[hh]) <= 8 else ((lane >= b_lanes[hh][0]) & (lane <= b_lanes[hh][-1]))
            qx.append(jnp.concatenate(
                [jnp.where((lane // HEAD_DIM) == hh, qa, jnp.zeros_like(qa)),
                 jnp.where(bmask, qb, jnp.zeros_like(qb))], axis=1))

    for idx in range(n_heads):
        m_ref[idx] = jnp.full((1, t), NEG_BIG, F32)
        acc_ref[idx] = jnp.zeros(acc_ref.shape[1:], F32)
    ones_rows = jnp.ones((ACC_ROWS - HEAD_DIM, t), BF16)

    def scores(kb, s_ref):
        ks = pl.multiple_of(kb * t, t)
        for p in range(pairs):
            kb_lo = 0 if kb_shared else p * LANES
            kx = jnp.concatenate([ka_ref[0, pl.ds(ks, t), p * LANES:(p + 1) * LANES],
                                  kb_ref[0, pl.ds(ks, t), kb_lo:kb_lo + LANES]], axis=1)
            for hh in range(2):
                s_ref[2 * p + hh] = _dot_nt(kx, qx[2 * p + hh])

    def softmax_pv(kb, s_ref, masked):
        ks = pl.multiple_of(kb * t, t)
        for idx in range(n_heads):
            st = s_ref[idx]
            if masked:
                kr = lax.broadcasted_iota(jnp.int32, (t, t), 0)
                qc = lax.broadcasted_iota(jnp.int32, (t, t), 1)
                st = jnp.where(kr <= qc, st, -jnp.inf)
            m_old = m_ref[idx]
            m_new = jnp.maximum(m_old, jnp.max(st, axis=0, keepdims=True))
            a = jnp.exp2(m_old - m_new)
            pt = jnp.exp2(st - m_new).astype(BF16)
            m_ref[idx] = m_new
            vt = jnp.concatenate([vt_ref[0, idx * HEAD_DIM:(idx + 1) * HEAD_DIM, pl.ds(ks, t)],
                                  ones_rows], axis=0)
            acc_ref[idx] = a * acc_ref[idx] + jnp.dot(vt, pt, preferred_element_type=F32)

    scores(0, se_ref)

    def body(i, carry):
        scores(2 * i + 1, so_ref)
        softmax_pv(2 * i, se_ref, False)
        scores(2 * i + 2, se_ref)
        softmax_pv(2 * i + 1, so_ref, False)
        return carry

    lax.fori_loop(0, qi // 2, body, 0)

    @pl.when(qi % 2 == 0)
    def _():
        softmax_pv(qi, se_ref, True)

    @pl.when(qi % 2 == 1)
    def _():
        scores(qi, so_ref)
        softmax_pv(qi - 1, se_ref, False)
        softmax_pv(qi, so_ref, True)

    def normalised(idx):
        acc = acc_ref[idx]
        return acc[:HEAD_DIM, :] * (1.0 / acc[HEAD_DIM:HEAD_DIM + 1, :])

    for p in range(pairs):
        ot = jnp.concatenate([normalised(2 * p), normalised(2 * p + 1)], axis=0)
        o_ref[0, :, p * LANES:(p + 1) * LANES] = ot.T.astype(BF16)


def _flash(qa, qb, ka, kb, vt, *, kb_shared, b_lanes):
    b, s, _ = qa.shape
    t = ATT_T
    pairs = ATT_PAIRS
    gw = pairs * LANES
    qspec = pl.BlockSpec((1, t, gw), lambda bi, g, qi: (bi, qi, g))
    kspec = pl.BlockSpec((1, s, gw), lambda bi, g, qi: (bi, 0, g))
    kbspec = pl.BlockSpec((1, s, LANES), lambda bi, g, qi: (bi, 0, 0)) if kb_shared else kspec
    vspec = pl.BlockSpec((1, gw, s), lambda bi, g, qi: (bi, g, 0))
    nh = 2 * pairs
    return pl.pallas_call(
        functools.partial(_flash_kernel, t=t, pairs=pairs, kb_shared=kb_shared, b_lanes=b_lanes),
        out_shape=jax.ShapeDtypeStruct((b, s, MIX_WIDTH), BF16),
        grid=(b, N_PAIRS // pairs, s // t),
        in_specs=[qspec, qspec, kspec, kbspec, vspec],
        out_specs=qspec,
        scratch_shapes=[pltpu.VMEM((nh, t, t), F32), pltpu.VMEM((nh, t, t), F32),
                        pltpu.VMEM((nh, 1, t), F32), pltpu.VMEM((nh, ACC_ROWS, t), F32)],
        compiler_params=_cparams(("arbitrary", "arbitrary", "arbitrary")),
        name="flash",
    )(qa, qb, ka, kb, vt)


def _mix_body(main_ref, mq_ref, mk_ref, mv_ref, wo_ref, x, g_ref, b_ref, alpha):
    lane = lax.broadcasted_iota(jnp.int32, (1, LANES), 1)
    first = (lane // HEAD_DIM) == 0
    mix = jnp.dot(main_ref[0], wo_ref[:MIX_WIDTH, :], preferred_element_type=F32)
    for pr in range(MEM_WIDTH // LANES):
        sl = slice(pr * LANES, (pr + 1) * LANES)
        q2 = mq_ref[0, :, sl]
        k2 = mk_ref[0, :, sl]
        v2 = mv_ref[0, :, sl]
        outs = []
        for hh in range(2):
            qm = jnp.where((lane // HEAD_DIM) == hh, q2, jnp.zeros_like(q2))
            s = _dot_nt(qm, k2)
            m = jnp.max(s, axis=-1, keepdims=True)
            p = jnp.exp(s - m)
            l = jnp.sum(p, axis=-1, keepdims=True)
            outs.append(jnp.dot(p.astype(BF16), v2, preferred_element_type=F32) / l)
        mo = jnp.where(first, outs[0], outs[1]).astype(BF16)
        mix = mix + jnp.dot(mo, wo_ref[MIX_WIDTH + pr * LANES:MIX_WIDTH + (pr + 1) * LANES, :],
                            preferred_element_type=F32)
    return _layer_norm(alpha * x + mix, g_ref[...], b_ref[...])


def _mix_ffn_kernel(main_ref, mq_ref, mk_ref, mv_ref, wo_ref, x_ref, g1_ref, b1_ref,
                    wg_ref, wu_ref, wd_ref, g2_ref, b2_ref, o_ref, *, alpha):
    x1 = _mix_body(main_ref, mq_ref, mk_ref, mv_ref, wo_ref, x_ref[0], g1_ref, b1_ref, alpha)
    o_ref[0] = _ffn_body(x1, wg_ref, wu_ref, wd_ref, g2_ref, b2_ref, alpha)


def _mix_ffn(main, mq, memkv, layer, wo, x, wts, alpha):
    bsz, s, d = x.shape
    tm = FFN_TM
    m = memkv.shape[1]
    ln_g, ln_b = wts[3], wts[4]
    tok = lambda w_: pl.BlockSpec((1, tm, w_), lambda bi, si: (bi, si, 0))
    specs, args = _ffn_specs(wts, layer, 1, 2)
    return pl.pallas_call(
        functools.partial(_mix_ffn_kernel, alpha=alpha),
        out_shape=jax.ShapeDtypeStruct((bsz, s, d), F32),
        grid=(bsz, s // tm),
        in_specs=[tok(MIX_WIDTH), tok(MEM_WIDTH),
                  pl.BlockSpec((1, m, MEM_WIDTH), lambda bi, si: (bi, 0, 2 * layer)),
                  pl.BlockSpec((1, m, MEM_WIDTH), lambda bi, si: (bi, 0, 2 * layer + 1)),
                  _layer_spec(wo, layer), tok(d), _layer_spec(ln_g, layer, 1),
                  _layer_spec(ln_b, layer, 1)] + specs,
        out_specs=tok(d),
        compiler_params=_cparams(("arbitrary", "arbitrary")),
        name="mix_ffn",
    )(main, mq, memkv, memkv, wo, x, ln_g, ln_b, *args)


def _pair_pad_cols(w3, lo, width):
    k = w3.shape[0]
    part = w3[:, :, lo:lo + width].reshape(k, N_PAIRS, 2 * width)
    return jnp.pad(part, ((0, 0), (0, 0), (0, LANES - 2 * width))).reshape(k, N_PAIRS * LANES)


def kernel(x, mem, positions, ln_g, ln_b, ffn_w_gate, ffn_w_up, ffn_w_down, mem_w_kv, w_o,
           fox_w_in, fox_b_f, pool_w_in, pool_w_grp, pool_scale, mla_w_in, mla_q_norm,
           mla_kv_norm, mla_w_uq, mla_w_ukv):
    bsz, s, d = x.shape
    depth = ln_g.shape[0]
    alpha = float((2 * depth) ** 0.25)
    att_scale = HEAD_DIM ** -0.5
    mw = MIX_WIDTH

    memkv = _mem_kv(mem, jnp.concatenate([mem_w_kv[i] for i in range(depth)], axis=1).astype(BF16))

    inv_freq = ROPE_THETA ** (-jnp.arange(0, QK_ROPE_DIM, 2, dtype=F32) / QK_ROPE_DIM)
    ang = positions.astype(F32)[..., None] * inv_freq
    cos = jnp.cos(ang)
    sin = jnp.sin(ang)
    zpad = ((0, 0), (0, 0), (0, LANES - 2 * QK_ROPE_DIM))
    cos_t = jnp.pad(jnp.concatenate([cos, cos, cos, cos], axis=-1), zpad)
    sin_t = jnp.pad(jnp.concatenate([-sin, sin, -sin, sin], axis=-1), zpad)

    wts = (ffn_w_gate.astype(BF16), ffn_w_up.astype(BF16), ffn_w_down.astype(BF16),
           ln_g[:, :, None, :], ln_b[:, :, None, :])
    wo_all = w_o.astype(BF16)

    for i in range(depth):
        kind, jj = i % N_MIXERS, i // N_MIXERS
        if kind == 0:
            w = fox_w_in[jj]
            w_main = jnp.concatenate([
                w[:, :mw] * (att_scale * LOG2E), w[:, mw:2 * mw],
                jnp.pad(w[:, 3 * mw:3 * mw + N_MIX_HEADS], ((0, 0), (0, LANES - N_MIX_HEADS))),
                w[:, 3 * mw + N_MIX_HEADS:] * att_scale], axis=1).astype(BF16)
            wvt = w[:, 2 * mw:3 * mw].T.astype(BF16)
            bf_pad = jnp.pad(fox_b_f[jj], (0, LANES - N_MIX_HEADS))[None]
            x, q, k, vt, qx, kx, mq = _ffn_fox_proj(x, wts, i, alpha, w_main, wvt, bf_pad)
            main = _flash(q, qx, k, kx, vt, kb_shared=False,
                          b_lanes=(_fox_ext_lanes(0), _fox_ext_lanes(1)))
        elif kind == 1:
            w = pool_w_in[jj]
            w = jnp.concatenate([w[:, :mw], w[:, mw:] * att_scale], axis=1).astype(BF16)
            w_bd = jax.scipy.linalg.block_diag(*[pool_w_grp[jj, gidx]
                                                 for gidx in range(len(POOL_WINDOWS))]).astype(BF16)
            x, main, mq = _ffn_pool_proj(x, wts, i, alpha, w, w_bd, pool_scale[jj][None])
        else:
            w = mla_w_in[jj]
            o2 = Q_LORA_RANK + KV_LORA_RANK
            kr = w[:, o2:o2 + QK_ROPE_DIM]
            w = jnp.concatenate([w[:, :o2], kr, kr,
                                 jnp.zeros((d, LANES - 2 * QK_ROPE_DIM), F32),
                                 w[:, o2 + QK_ROPE_DIM:] * att_scale], axis=1).astype(BF16)
            qd = QK_NOPE_DIM + QK_ROPE_DIM
            wuq3 = mla_w_uq[jj].reshape(Q_LORA_RANK, N_MIX_HEADS, qd)
            wuq = jnp.concatenate([wuq3[:, :, :QK_NOPE_DIM].reshape(Q_LORA_RANK, mw),
                                   _pair_pad_cols(wuq3, QK_NOPE_DIM, QK_ROPE_DIM)],
                                  axis=1).astype(BF16)
            kvd = QK_NOPE_DIM + V_HEAD_DIM
            wukv3 = mla_w_ukv[jj].reshape(KV_LORA_RANK, N_MIX_HEADS, kvd)
            wuk = wukv3[:, :, :QK_NOPE_DIM].reshape(KV_LORA_RANK, mw).astype(BF16)
            wuvt = wukv3[:, :, QK_NOPE_DIM:].reshape(KV_LORA_RANK, mw).T.astype(BF16)
            x, qn, qpe, kn, kpe, vt, mq = _ffn_mla_proj(
                x, wts, i, alpha, w, mla_q_norm[jj][None], mla_kv_norm[jj][None], wuq, wuk, wuvt,
                cos_t, sin_t, float(qd ** -0.5) * LOG2E)
            rope_lanes = (tuple(range(QK_ROPE_DIM)), tuple(range(QK_ROPE_DIM, 2 * QK_ROPE_DIM)))
            main = _flash(qn, qpe, kn, kpe, vt, kb_shared=True, b_lanes=rope_lanes)
        x = _mix_ffn(main, mq, memkv, i, wo_all, x, wts, alpha)
    return x
```

```python
import functools
import math

import numpy as np
import jax
import jax.numpy as jnp
from jax import lax
from jax.experimental import pallas as pl
from jax.experimental.pallas import tpu as pltpu

F32 = jnp.float32
BF16 = jnp.bfloat16

N_MIXERS = 3
HEAD_DIM = 64
MIX_WIDTH = 768
N_MIX_HEADS = MIX_WIDTH // HEAD_DIM
N_MEM_HEADS = 4
MEM_WIDTH = N_MEM_HEADS * HEAD_DIM
POOL_WINDOWS = (2, 4, 8, 16)
POOL_GROUP_DIM = MIX_WIDTH // len(POOL_WINDOWS)
Q_LORA_RANK = 384
KV_LORA_RANK = 256
QK_NOPE_DIM = 64
QK_ROPE_DIM = 32
V_HEAD_DIM = 64
ROPE_THETA = 10000.0
LN_EPS = 1e-5
RMS_EPS = 1e-6

LANES = 128
VMEM_LIMIT_BYTES = 56 * 1024 * 1024

FFN_TM = 512
ATT_T = 256
ATT_PAIRS = 6
N_PAIRS = N_MIX_HEADS // 2
ACC_ROWS = HEAD_DIM + 16
NEG_BIG = -1e30
LOG2E = math.log2(math.e)
N_SPLIT = 3


def _cparams(sem):
    return pltpu.CompilerParams(dimension_semantics=sem, vmem_limit_bytes=VMEM_LIMIT_BYTES)


def _const_spec(shape):
    nd = len(shape)
    return pl.BlockSpec(shape, lambda *_: (0,) * nd, pipeline_mode=pl.Buffered(1))


def _layer_norm(y, g, b):
    mu = jnp.mean(y, axis=-1, keepdims=True)
    d = y - mu
    var = jnp.mean(d * d, axis=-1, keepdims=True)
    return d * lax.rsqrt(var + LN_EPS) * g + b


def _rms_norm(y, g):
    ms = jnp.mean(y * y, axis=-1, keepdims=True)
    return y * lax.rsqrt(ms + RMS_EPS) * g


def _dot_nt(a, b):
    return lax.dot_general(a, b, (((1,), (1,)), ((), ())), preferred_element_type=F32)


def _ffn_body(x, wg_ref, wu_ref, wd_ref, g_ref, b_ref, alpha):
    xb = x.astype(BF16)
    gate = jnp.dot(xb, wg_ref[...], preferred_element_type=F32)
    up = jnp.dot(xb, wu_ref[...], preferred_element_type=F32)
    h = ((gate * jax.nn.sigmoid(gate)) * up).astype(BF16)
    ff = jnp.dot(h, wd_ref[...], preferred_element_type=F32)
    return _layer_norm(alpha * x + 0.5 * ff, g_ref[...], b_ref[...])


def _layer_spec(arr, *idx):
    lead = len(idx)
    shape = (None,) * lead + tuple(arr.shape[lead:])
    tail = (0,) * (arr.ndim - lead)
    return pl.BlockSpec(shape, lambda *_: tuple(idx) + tail, pipeline_mode=pl.Buffered(1))


def _ffn_specs(wts, layer, half, ln_idx):
    wg, wu, wd, ln_g, ln_b = wts
    return ([_layer_spec(wg, layer, half), _layer_spec(wu, layer, half),
             _layer_spec(wd, layer, half), _layer_spec(ln_g, layer, ln_idx),
             _layer_spec(ln_b, layer, ln_idx)], [wg, wu, wd, ln_g, ln_b])


def _ffn_proj_call(name, proj_kernel, x, wts, layer, alpha, proj_args, proj_specs, out_shape,
                   out_specs, scratch=()):
    bsz, s, d = x.shape
    tm = FFN_TM
    tok = pl.BlockSpec((1, tm, d), lambda bi, si: (bi, si, 0))
    fspecs, fargs = _ffn_specs(wts, layer, 0, 0)
    n_in = len(proj_args)

    def fused(x_ref, wg_ref, wu_ref, wd_ref, g_ref, b_ref, *rest):
        x1 = _ffn_body(x_ref[0], wg_ref, wu_ref, wd_ref, g_ref, b_ref, alpha)
        rest[n_in][0] = x1
        proj_kernel(x1, *rest[:n_in], *rest[n_in + 1:])

    return pl.pallas_call(
        fused,
        out_shape=(jax.ShapeDtypeStruct((bsz, s, d), F32),) + tuple(out_shape),
        grid=(bsz, s // tm),
        in_specs=[tok] + fspecs + list(proj_specs),
        out_specs=(tok,) + tuple(out_specs),
        scratch_shapes=list(scratch),
        compiler_params=_cparams(("arbitrary", "arbitrary")),
        name=name,
    )(x, *fargs, *proj_args)


def _memkv_kernel(m_ref, w_ref, o_ref):
    o_ref[0] = jnp.dot(m_ref[0].astype(BF16), w_ref[...], preferred_element_type=F32).astype(BF16)


def _mem_kv(mem, w_all):
    b, m, d = mem.shape
    n = w_all.shape[1]
    return pl.pallas_call(
        _memkv_kernel,
        out_shape=jax.ShapeDtypeStruct((b, m, n), BF16),
        grid=(b,),
        in_specs=[pl.BlockSpec((1, m, d), lambda i: (i, 0, 0)), _const_spec(w_all.shape)],
        out_specs=pl.BlockSpec((1, m, n), lambda i: (i, 0, 0)),
        compiler_params=_cparams(("arbitrary",)),
        name="mem_kv",
    )(mem, w_all)


def _decay_placement():
    pq = np.zeros((N_SPLIT * LANES, MIX_WIDTH), np.float32)
    pk = np.zeros((N_SPLIT * LANES, MIX_WIDTH), np.float32)
    cq = np.zeros((1, MIX_WIDTH), np.float32)
    ck = np.zeros((1, MIX_WIDTH), np.float32)
    for j in range(N_PAIRS):
        base = j * LANES
        ck[0, base:base + 2 * N_SPLIT] = 1.0
        cq[0, base + 2 * N_SPLIT:base + 4 * N_SPLIT] = -1.0
        for hh in range(2):
            for c in range(N_SPLIT):
                pq[c * LANES + 2 * j + hh, base + hh * N_SPLIT + c] = 1.0
                pk[c * LANES + 2 * j + hh, base + 2 * N_SPLIT + hh * N_SPLIT + c] = 1.0
    return pq, pk, cq, ck


def _fox_ext_lanes(hh):
    return tuple(range(hh * N_SPLIT, (hh + 1) * N_SPLIT)) + tuple(
        range(2 * N_SPLIT + hh * N_SPLIT, 2 * N_SPLIT + (hh + 1) * N_SPLIT))


def _fox_proj_kernel(x, w_ref, wvt_ref, bf_ref, pq_ref, pk_ref, cq_ref, ck_ref,
                     q_ref, k_ref, vt_ref, qx_ref, kx_ref, mq_ref, carry_ref, *, tm):
    si = pl.program_id(1)
    xb = x.astype(BF16)
    h = jnp.dot(xb, w_ref[...], preferred_element_type=F32)
    mw = MIX_WIDTH
    q_ref[0] = h[:, :mw].astype(BF16)
    k_ref[0] = h[:, mw:2 * mw].astype(BF16)
    mq_ref[0] = h[:, 2 * mw + LANES:].astype(BF16)
    vt_ref[0] = _dot_nt(wvt_ref[...], xb).astype(BF16)

    lf = jax.nn.log_sigmoid(h[:, 2 * mw:2 * mw + LANES] + bf_ref[...])
    row = lax.broadcasted_iota(jnp.int32, (tm, LANES), 0)
    shift = 1
    while shift < tm:
        lf = lf + jnp.where(row >= shift, pltpu.roll(lf, shift, 0), 0.0)
        shift *= 2

    @pl.when(si == 0)
    def _():
        carry_ref[...] = jnp.zeros_like(carry_ref)

    cum = lf + carry_ref[0:1, :]
    carry_ref[...] = jnp.broadcast_to(cum[tm - 1:tm, :], carry_ref.shape)

    parts = []
    rem = cum * LOG2E
    for _ in range(N_SPLIT):
        part = rem.astype(BF16)
        parts.append(part)
        rem = rem - part.astype(F32)
    split = jnp.concatenate(parts, axis=1)
    qx_ref[0] = (jnp.dot(split, pq_ref[...], preferred_element_type=F32)
                 + cq_ref[...]).astype(BF16)
    kx_ref[0] = (jnp.dot(split, pk_ref[...], preferred_element_type=F32)
                 + ck_ref[...]).astype(BF16)


def _ffn_fox_proj(x, wts, layer, alpha, jj, w_all, wvt_all, bf_all):
    b, s, d = x.shape
    tm = FFN_TM
    pq, pk, cq, ck = _decay_placement()
    pq = jnp.asarray(pq, BF16)
    pk = jnp.asarray(pk, BF16)
    cq = jnp.asarray(cq)
    ck = jnp.asarray(ck)
    tok = lambda w_: pl.BlockSpec((1, tm, w_), lambda bi, si: (bi, si, 0))
    sd = lambda w_: jax.ShapeDtypeStruct((b, s, w_), BF16)
    stacked = (w_all, wvt_all, bf_all)
    consts = (pq, pk, cq, ck)
    return _ffn_proj_call(
        "ffn_fox_proj", functools.partial(_fox_proj_kernel, tm=tm), x, wts, layer, alpha,
        stacked + consts,
        [_layer_spec(a, jj) for a in stacked] + [_const_spec(a.shape) for a in consts],
        (sd(MIX_WIDTH), sd(MIX_WIDTH), jax.ShapeDtypeStruct((b, MIX_WIDTH, s), BF16),
         sd(MIX_WIDTH), sd(MIX_WIDTH), sd(MEM_WIDTH)),
        (tok(MIX_WIDTH), tok(MIX_WIDTH),
         pl.BlockSpec((1, MIX_WIDTH, tm), lambda bi, si: (bi, 0, si)),
         tok(MIX_WIDTH), tok(MIX_WIDTH), tok(MEM_WIDTH)),
        scratch=[pltpu.VMEM((8, LANES), F32)])


def _pool_kernel(x, w_ref, wg_ref, sc_ref, main_ref, mq_ref, carry_ref, *, tm):
    si = pl.program_id(1)
    halo = carry_ref.shape[0]
    xb = x.astype(BF16)
    h = jnp.dot(xb, w_ref[...], preferred_element_type=F32)
    u = h[:, :MIX_WIDTH]
    mq_ref[0] = h[:, MIX_WIDTH:].astype(BF16)

    @pl.when(si == 0)
    def _():
        carry_ref[...] = jnp.zeros_like(carry_ref)

    ext = jnp.concatenate([carry_ref[...], u], axis=0)
    carry_ref[...] = u[tm - halo:, :]

    lane = lax.broadcasted_iota(jnp.int32, (1, MIX_WIDTH), 1)
    t_pos = si * tm + lax.broadcasted_iota(jnp.int32, (tm, 1), 0)
    acc = ext
    pooled = None
    span = 1
    for gi, win in enumerate(POOL_WINDOWS):
        while span < win:
            acc = acc + pltpu.roll(acc, span, 0)
            span *= 2
        cnt = jnp.minimum(t_pos + 1, win).astype(F32)
        mean = acc[halo:, :] / cnt
        if pooled is None:
            pooled = mean
        else:
            pooled = jnp.where(lane >= gi * POOL_GROUP_DIM, mean, pooled)
    pooled = pooled - u
    y = jnp.dot(pooled.astype(BF16), wg_ref[...], preferred_element_type=F32)
    main_ref[0] = (y * sc_ref[...]).astype(BF16)


def _ffn_pool_proj(x, wts, layer, alpha, w, w_bd, scale):
    b, s, d = x.shape
    tm = FFN_TM
    halo = 16
    tok = lambda w_: pl.BlockSpec((1, tm, w_), lambda bi, si: (bi, si, 0))
    args = (w, w_bd, scale)
    return _ffn_proj_call(
        "ffn_pool_proj", functools.partial(_pool_kernel, tm=tm), x, wts, layer, alpha, args,
        [_const_spec(a.shape) for a in args],
        (jax.ShapeDtypeStruct((b, s, MIX_WIDTH), BF16), jax.ShapeDtypeStruct((b, s, MEM_WIDTH), BF16)),
        (tok(MIX_WIDTH), tok(MEM_WIDTH)),
        scratch=[pltpu.VMEM((halo, MIX_WIDTH), F32)])


def _rope(xv, cos_t, sin_t, lane):
    half = QK_ROPE_DIM // 2
    swapped = jnp.where((lane % QK_ROPE_DIM) < half,
                        pltpu.roll(xv, LANES - half, 1), pltpu.roll(xv, half, 1))
    return xv * cos_t + swapped * sin_t


def _mla_proj_kernel(x, w_ref, qn_ref, kvn_ref, wuq_ref, wuk_ref, wuvt_ref, cos_ref, sin_ref,
                     qnope_ref, qpe_ref, knope_ref, kpe_ref, vt_ref, mq_ref, *, scale):
    xb = x.astype(BF16)
    h = jnp.dot(xb, w_ref[...], preferred_element_type=F32)
    o1 = Q_LORA_RANK
    o2 = o1 + KV_LORA_RANK
    o3 = o2 + LANES
    cq = _rms_norm(h[:, :o1], qn_ref[...]).astype(BF16)
    ckv = _rms_norm(h[:, o1:o2], kvn_ref[...]).astype(BF16)
    mq_ref[0] = h[:, o3:].astype(BF16)
    cos_t = cos_ref[0]
    sin_t = sin_ref[0]
    lane = lax.broadcasted_iota(jnp.int32, (1, LANES), 1)
    kpe_ref[0] = _rope(h[:, o2:o3], cos_t, sin_t, lane).astype(BF16)

    q = jnp.dot(cq, wuq_ref[...], preferred_element_type=F32) * scale
    qnope_ref[0] = q[:, :MIX_WIDTH].astype(BF16)
    for blk in range(N_PAIRS):
        lo = MIX_WIDTH + blk * LANES
        qpe_ref[0, :, blk * LANES:(blk + 1) * LANES] = _rope(
            q[:, lo:lo + LANES], cos_t, sin_t, lane).astype(BF16)
    knope_ref[0] = jnp.dot(ckv, wuk_ref[...], preferred_element_type=F32).astype(BF16)
    vt_ref[0] = _dot_nt(wuvt_ref[...], ckv).astype(BF16)


def _ffn_mla_proj(x, wts, layer, alpha, w, qn, kvn, wuq, wuk, wuvt, cos_t, sin_t, scale):
    b, s, d = x.shape
    tm = FFN_TM
    tok = lambda w_: pl.BlockSpec((1, tm, w_), lambda bi, si: (bi, si, 0))
    sd = lambda w_: jax.ShapeDtypeStruct((b, s, w_), BF16)
    consts = (w, qn, kvn, wuq, wuk, wuvt)
    return _ffn_proj_call(
        "ffn_mla_proj", functools.partial(_mla_proj_kernel, scale=scale), x, wts, layer, alpha,
        consts + (cos_t, sin_t),
        [_const_spec(a.shape) for a in consts] + [tok(LANES), tok(LANES)],
        (sd(MIX_WIDTH), sd(MIX_WIDTH), sd(MIX_WIDTH), sd(LANES),
         jax.ShapeDtypeStruct((b, MIX_WIDTH, s), BF16), sd(MEM_WIDTH)),
        (tok(MIX_WIDTH), tok(MIX_WIDTH), tok(MIX_WIDTH), tok(LANES),
         pl.BlockSpec((1, MIX_WIDTH, tm), lambda bi, si: (bi, 0, si)), tok(MEM_WIDTH)))


def _flash_kernel(qa_ref, qb_ref, ka_ref, kb_ref, vt_ref, o_ref, se_ref, so_ref, me_ref, mo_ref,
                  m_ref, acc_ref, *, t, pairs, kb_shared, b_lanes):
    qi = pl.program_id(2)
    lane = lax.broadcasted_iota(jnp.int32, (1, LANES), 1)
    n_heads = 2 * pairs

    qx = []
    for p in range(pairs):
        qa = qa_ref[0, :, p * LANES:(p + 1) * LANES]
        qb = qb_ref[0, :, p * LANES:(p + 1) * LANES]
        for hh in range(2):
            bmask = functools.reduce(jnp.logical_or, [lane == ln for ln in b_lanes[hh]]) \
                if len(b_lanes[hh]) <= 8 else ((lane >= b_lanes[hh][0]) & (lane <= b_lanes[hh][-1]))
            qx.append(jnp.concatenate(
                [jnp.where((lane // HEAD_DIM) == hh, qa, jnp.zeros_like(qa)),
                 jnp.where(bmask, qb, jnp.zeros_like(qb))], axis=1))

    for idx in range(n_heads):
        m_ref[idx] = jnp.full((1, t), NEG_BIG, F32)
        acc_ref[idx] = jnp.zeros(acc_ref.shape[1:], F32)
    ones_rows = jnp.ones((ACC_ROWS - HEAD_DIM, t), BF16)

    def scores(kb, bufs):
        s_ref, mx_ref = bufs
        ks = pl.multiple_of(kb * t, t)
        for p in range(pairs):
            kb_lo = 0 if kb_shared else p * LANES
            kx = jnp.concatenate([ka_ref[0, pl.ds(ks, t), p * LANES:(p + 1) * LANES],
                                  kb_ref[0, pl.ds(ks, t), kb_lo:kb_lo + LANES]], axis=1)
            for hh in range(2):
                st = _dot_nt(kx, qx[2 * p + hh])
                s_ref[2 * p + hh] = st
                mx_ref[2 * p + hh] = jnp.max(st, axis=0, keepdims=True)

    def softmax_pv(kb, bufs, masked):
        s_ref, mx_ref = bufs
        ks = pl.multiple_of(kb * t, t)
        for idx in range(n_heads):
            st = s_ref[idx]
            if masked:
                kr = lax.broadcasted_iota(jnp.int32, (t, t), 0)
                qc = lax.broadcasted_iota(jnp.int32, (t, t), 1)
                st = jnp.where(kr <= qc, st, -jnp.inf)
                blk_max = jnp.max(st, axis=0, keepdims=True)
            else:
                blk_max = mx_ref[idx]
            m_old = m_ref[idx]
            m_new = jnp.maximum(m_old, blk_max)
            a = jnp.exp2(m_old - m_new)
            pt = jnp.exp2(st - m_new).astype(BF16)
            m_ref[idx] = m_new
            vt = jnp.concatenate([vt_ref[0, idx * HEAD_DIM:(idx + 1) * HEAD_DIM, pl.ds(ks, t)],
                                  ones_rows], axis=0)
            acc_ref[idx] = a * acc_ref[idx] + jnp.dot(vt, pt, preferred_element_type=F32)

    even = (se_ref, me_ref)
    odd = (so_ref, mo_ref)
    scores(0, even)

    def body(i, carry):
        scores(2 * i + 1, odd)
        softmax_pv(2 * i, even, False)
        scores(2 * i + 2, even)
        softmax_pv(2 * i + 1, odd, False)
        return carry

    lax.fori_loop(0, qi // 2, body, 0)

    @pl.when(qi % 2 == 0)
    def _():
        softmax_pv(qi, even, True)

    @pl.when(qi % 2 == 1)
    def _():
        scores(qi, odd)
        softmax_pv(qi - 1, even, False)
        softmax_pv(qi, odd, True)

    def normalised(idx):
        acc = acc_ref[idx]
        return acc[:HEAD_DIM, :] * (1.0 / acc[HEAD_DIM:HEAD_DIM + 1, :])

    for p in range(pairs):
        ot = jnp.concatenate([normalised(2 * p), normalised(2 * p + 1)], axis=0)
        o_ref[0, :, p * LANES:(p + 1) * LANES] = ot.T.astype(BF16)


def _flash(qa, qb, ka, kb, vt, *, kb_shared, b_lanes):
    b, s, _ = qa.shape
    t = ATT_T
    pairs = ATT_PAIRS
    gw = pairs * LANES
    qspec = pl.BlockSpec((1, t, gw), lambda bi, g, qi: (bi, qi, g))
    kspec = pl.BlockSpec((1, s, gw), lambda bi, g, qi: (bi, 0, g))
    kbspec = pl.BlockSpec((1, s, LANES), lambda bi, g, qi: (bi, 0, 0)) if kb_shared else kspec
    vspec = pl.BlockSpec((1, gw, s), lambda bi, g, qi: (bi, g, 0))
    nh = 2 * pairs
    return pl.pallas_call(
        functools.partial(_flash_kernel, t=t, pairs=pairs, kb_shared=kb_shared, b_lanes=b_lanes),
        out_shape=jax.ShapeDtypeStruct((b, s, MIX_WIDTH), BF16),
        grid=(b, N_PAIRS // pairs, s // t),
        in_specs=[qspec, qspec, kspec, kbspec, vspec],
        out_specs=qspec,
        scratch_shapes=[pltpu.VMEM((nh, t, t), F32), pltpu.VMEM((nh, t, t), F32),
                        pltpu.VMEM((nh, 1, t), F32), pltpu.VMEM((nh, 1, t), F32),
                        pltpu.VMEM((nh, 1, t), F32), pltpu.VMEM((nh, ACC_ROWS, t), F32)],
        compiler_params=_cparams(("arbitrary", "arbitrary", "arbitrary")),
        name="flash",
    )(qa, qb, ka, kb, vt)


def _mix_body(main_ref, mq_ref, mk_ref, mv_ref, wo_ref, x, g_ref, b_ref, alpha):
    lane = lax.broadcasted_iota(jnp.int32, (1, LANES), 1)
    first = (lane // HEAD_DIM) == 0
    mix = jnp.dot(main_ref[0], wo_ref[:MIX_WIDTH, :], preferred_element_type=F32)
    for pr in range(MEM_WIDTH // LANES):
        sl = slice(pr * LANES, (pr + 1) * LANES)
        q2 = mq_ref[0, :, sl]
        k2 = mk_ref[0, :, sl]
        v2 = mv_ref[0, :, sl]
        outs = []
        for hh in range(2):
            qm = jnp.where((lane // HEAD_DIM) == hh, q2, jnp.zeros_like(q2))
            s = _dot_nt(qm, k2)
            m = jnp.max(s, axis=-1, keepdims=True)
            p = jnp.exp(s - m)
            l = jnp.sum(p, axis=-1, keepdims=True)
            outs.append(jnp.dot(p.astype(BF16), v2, preferred_element_type=F32) / l)
        mo = jnp.where(first, outs[0], outs[1]).astype(BF16)
        mix = mix + jnp.dot(mo, wo_ref[MIX_WIDTH + pr * LANES:MIX_WIDTH + (pr + 1) * LANES, :],
                            preferred_element_type=F32)
    return _layer_norm(alpha * x + mix, g_ref[...], b_ref[...])


def _mix_ffn_kernel(main_ref, mq_ref, mk_ref, mv_ref, wo_ref, x_ref, g1_ref, b1_ref,
                    wg_ref, wu_ref, wd_ref, g2_ref, b2_ref, o_ref, *, alpha):
    x1 = _mix_body(main_ref, mq_ref, mk_ref, mv_ref, wo_ref, x_ref[0], g1_ref, b1_ref, alpha)
    o_ref[0] = _ffn_body(x1, wg_ref, wu_ref, wd_ref, g2_ref, b2_ref, alpha)


def _mix_ffn(main, mq, memkv, layer, wo, x, wts, alpha):
    bsz, s, d = x.shape
    tm = FFN_TM
    m = memkv.shape[1]
    ln_g, ln_b = wts[3], wts[4]
    tok = lambda w_: pl.BlockSpec((1, tm, w_), lambda bi, si: (bi, si, 0))
    specs, args = _ffn_specs(wts, layer, 1, 2)
    return pl.pallas_call(
        functools.partial(_mix_ffn_kernel, alpha=alpha),
        out_shape=jax.ShapeDtypeStruct((bsz, s, d), F32),
        grid=(bsz, s // tm),
        in_specs=[tok(MIX_WIDTH), tok(MEM_WIDTH),
                  pl.BlockSpec((1, m, MEM_WIDTH), lambda bi, si: (bi, 0, 2 * layer)),
                  pl.BlockSpec((1, m, MEM_WIDTH), lambda bi, si: (bi, 0, 2 * layer + 1)),
                  _layer_spec(wo, layer), tok(d), _layer_spec(ln_g, layer, 1),
                  _layer_spec(ln_b, layer, 1)] + specs,
        out_specs=tok(d),
        compiler_params=_cparams(("arbitrary", "arbitrary")),
        name="mix_ffn",
    )(main, mq, memkv, memkv, wo, x, ln_g, ln_b, *args)


def _pair_pad_cols(w3, lo, width):
    k = w3.shape[0]
    part = w3[:, :, lo:lo + width].reshape(k, N_PAIRS, 2 * width)
    return jnp.pad(part, ((0, 0), (0, 0), (0, LANES - 2 * width))).reshape(k, N_PAIRS * LANES)


def kernel(x, mem, positions, ln_g, ln_b, ffn_w_gate, ffn_w_up, ffn_w_down, mem_w_kv, w_o,
           fox_w_in, fox_b_f, pool_w_in, pool_w_grp, pool_scale, mla_w_in, mla_q_norm,
           mla_kv_norm, mla_w_uq, mla_w_ukv):
    bsz, s, d = x.shape
    depth = ln_g.shape[0]
    alpha = float((2 * depth) ** 0.25)
    att_scale = HEAD_DIM ** -0.5
    mw = MIX_WIDTH

    memkv = _mem_kv(mem, jnp.concatenate([mem_w_kv[i] for i in range(depth)], axis=1).astype(BF16))

    inv_freq = ROPE_THETA ** (-jnp.arange(0, QK_ROPE_DIM, 2, dtype=F32) / QK_ROPE_DIM)
    half = QK_ROPE_DIM // 2
    lane_id = np.arange(LANES)
    live = jnp.asarray(lane_id < 2 * QK_ROPE_DIM)
    sign = jnp.asarray(np.where((lane_id % QK_ROPE_DIM) < half, -1.0, 1.0).astype(np.float32))
    ang = positions.astype(F32)[..., None] * jnp.tile(inv_freq, LANES // half)
    cos_t = jnp.where(live, jnp.cos(ang), 0.0)
    sin_t = jnp.where(live, sign * jnp.sin(ang), 0.0)

    wts = (ffn_w_gate.astype(BF16), ffn_w_up.astype(BF16), ffn_w_down.astype(BF16),
           ln_g[:, :, None, :], ln_b[:, :, None, :])
    wo_all = w_o.astype(BF16)

    gate_pad = ((0, 0), (0, 0), (0, LANES - N_MIX_HEADS))
    fox_w = jnp.concatenate([
        fox_w_in[:, :, :mw] * (att_scale * LOG2E), fox_w_in[:, :, mw:2 * mw],
        jnp.pad(fox_w_in[:, :, 3 * mw:3 * mw + N_MIX_HEADS], gate_pad),
        fox_w_in[:, :, 3 * mw + N_MIX_HEADS:] * att_scale], axis=2).astype(BF16)
    fox_wvt = jnp.swapaxes(fox_w_in[:, :, 2 * mw:3 * mw], 1, 2).astype(BF16)
    fox_bf = jnp.pad(fox_b_f, ((0, 0), (0, LANES - N_MIX_HEADS)))[:, None, :]

    for i in range(depth):
        kind, jj = i % N_MIXERS, i // N_MIXERS
        if kind == 0:
            x, q, k, vt, qx, kx, mq = _ffn_fox_proj(x, wts, i, alpha, jj, fox_w, fox_wvt, fox_bf)
            main = _flash(q, qx, k, kx, vt, kb_shared=False,
                          b_lanes=(_fox_ext_lanes(0), _fox_ext_lanes(1)))
        elif kind == 1:
            w = pool_w_in[jj]
            w = jnp.concatenate([w[:, :mw], w[:, mw:] * att_scale], axis=1).astype(BF16)
            w_bd = jax.scipy.linalg.block_diag(*[pool_w_grp[jj, gidx]
                                                 for gidx in range(len(POOL_WINDOWS))]).astype(BF16)
            x, main, mq = _ffn_pool_proj(x, wts, i, alpha, w, w_bd, pool_scale[jj][None])
        else:
            w = mla_w_in[jj]
            o2 = Q_LORA_RANK + KV_LORA_RANK
            kr = w[:, o2:o2 + QK_ROPE_DIM]
            w = jnp.concatenate([w[:, :o2], kr, kr,
                                 jnp.zeros((d, LANES - 2 * QK_ROPE_DIM), F32),
                                 w[:, o2 + QK_ROPE_DIM:] * att_scale], axis=1).astype(BF16)
            qd = QK_NOPE_DIM + QK_ROPE_DIM
            wuq3 = mla_w_uq[jj].reshape(Q_LORA_RANK, N_MIX_HEADS, qd)
            wuq = jnp.concatenate([wuq3[:, :, :QK_NOPE_DIM].reshape(Q_LORA_RANK, mw),
                                   _pair_pad_cols(wuq3, QK_NOPE_DIM, QK_ROPE_DIM)],
                                  axis=1).astype(BF16)
            kvd = QK_NOPE_DIM + V_HEAD_DIM
            wukv3 = mla_w_ukv[jj].reshape(KV_LORA_RANK, N_MIX_HEADS, kvd)
            wuk = wukv3[:, :, :QK_NOPE_DIM].reshape(KV_LORA_RANK, mw).astype(BF16)
            wuvt = wukv3[:, :, QK_NOPE_DIM:].reshape(KV_LORA_RANK, mw).T.astype(BF16)
            x, qn, qpe, kn, kpe, vt, mq = _ffn_mla_proj(
                x, wts, i, alpha, w, mla_q_norm[jj][None], mla_kv_norm[jj][None], wuq, wuk, wuvt,
                cos_t, sin_t, float(qd ** -0.5) * LOG2E)
            rope_lanes = (tuple(range(QK_ROPE_DIM)), tuple(range(QK_ROPE_DIM, 2 * QK_ROPE_DIM)))
            main = _flash(qn, qpe, kn, kpe, vt, kb_shared=True, b_lanes=rope_lanes)
        x = _mix_ffn(main, mq, memkv, i, wo_all, x, wts, alpha)
    return x
```

```python
import functools
import math

import numpy as np
import jax
import jax.numpy as jnp
from jax import lax
from jax.experimental import pallas as pl
from jax.experimental.pallas import tpu as pltpu

F32 = jnp.float32
BF16 = jnp.bfloat16

N_MIXERS = 3
HEAD_DIM = 64
MIX_WIDTH = 768
N_MIX_HEADS = MIX_WIDTH // HEAD_DIM
N_MEM_HEADS = 4
MEM_WIDTH = N_MEM_HEADS * HEAD_DIM
POOL_WINDOWS = (2, 4, 8, 16)
POOL_GROUP_DIM = MIX_WIDTH // len(POOL_WINDOWS)
Q_LORA_RANK = 384
KV_LORA_RANK = 256
QK_NOPE_DIM = 64
QK_ROPE_DIM = 32
V_HEAD_DIM = 64
ROPE_THETA = 10000.0
LN_EPS = 1e-5
RMS_EPS = 1e-6

LANES = 128
VMEM_LIMIT_BYTES = 56 * 1024 * 1024

FFN_TM = 512
ATT_T = 256
ATT_PAIRS = 6
N_PAIRS = N_MIX_HEADS // 2
ACC_ROWS = HEAD_DIM + 16
NEG_BIG = -1e30
LOG2E = math.log2(math.e)
N_SPLIT = 3


def _cparams(sem):
    return pltpu.CompilerParams(dimension_semantics=sem, vmem_limit_bytes=VMEM_LIMIT_BYTES)


def _const_spec(shape):
    nd = len(shape)
    return pl.BlockSpec(shape, lambda *_: (0,) * nd, pipeline_mode=pl.Buffered(1))


def _layer_norm(y, g, b):
    mu = jnp.mean(y, axis=-1, keepdims=True)
    d = y - mu
    var = jnp.mean(d * d, axis=-1, keepdims=True)
    return d * lax.rsqrt(var + LN_EPS) * g + b


def _rms_norm(y, g):
    ms = jnp.mean(y * y, axis=-1, keepdims=True)
    return y * lax.rsqrt(ms + RMS_EPS) * g


def _dot_nt(a, b):
    return lax.dot_general(a, b, (((1,), (1,)), ((), ())), preferred_element_type=F32)


def _ffn_body(x, wg_ref, wu_ref, wd_ref, g_ref, b_ref, alpha):
    xb = x.astype(BF16)
    gate = jnp.dot(xb, wg_ref[...], preferred_element_type=F32)
    up = jnp.dot(xb, wu_ref[...], preferred_element_type=F32)
    h = ((gate * jax.nn.sigmoid(gate)) * up).astype(BF16)
    ff = jnp.dot(h, wd_ref[...], preferred_element_type=F32)
    return _layer_norm(alpha * x + 0.5 * ff, g_ref[...], b_ref[...])


def _layer_spec(arr, *idx):
    lead = len(idx)
    shape = (None,) * lead + tuple(arr.shape[lead:])
    tail = (0,) * (arr.ndim - lead)
    return pl.BlockSpec(shape, lambda *_: tuple(idx) + tail, pipeline_mode=pl.Buffered(1))


def _ffn_specs(wts, layer, half, ln_idx):
    wg, wu, wd, ln_g, ln_b = wts
    return ([_layer_spec(wg, layer, half), _layer_spec(wu, layer, half),
             _layer_spec(wd, layer, half), _layer_spec(ln_g, layer, ln_idx),
             _layer_spec(ln_b, layer, ln_idx)], [wg, wu, wd, ln_g, ln_b])


def _ffn_proj_call(name, proj_kernel, x, wts, layer, alpha, proj_args, proj_specs, out_shape,
                   out_specs, scratch=()):
    bsz, s, d = x.shape
    tm = FFN_TM
    tok = pl.BlockSpec((1, tm, d), lambda bi, si: (bi, si, 0))
    fspecs, fargs = _ffn_specs(wts, layer, 0, 0)
    n_in = len(proj_args)

    def fused(x_ref, wg_ref, wu_ref, wd_ref, g_ref, b_ref, *rest):
        x1 = _ffn_body(x_ref[0], wg_ref, wu_ref, wd_ref, g_ref, b_ref, alpha)
        rest[n_in][0] = x1
        proj_kernel(x1, *rest[:n_in], *rest[n_in + 1:])

    return pl.pallas_call(
        fused,
        out_shape=(jax.ShapeDtypeStruct((bsz, s, d), F32),) + tuple(out_shape),
        grid=(bsz, s // tm),
        in_specs=[tok] + fspecs + list(proj_specs),
        out_specs=(tok,) + tuple(out_specs),
        scratch_shapes=list(scratch),
        compiler_params=_cparams(("arbitrary", "arbitrary")),
        name=name,
    )(x, *fargs, *proj_args)


def _memkv_kernel(m_ref, w_ref, o_ref):
    o_ref[0] = jnp.dot(m_ref[0].astype(BF16), w_ref[...], preferred_element_type=F32).astype(BF16)


def _mem_kv(mem, w_all):
    b, m, d = mem.shape
    n = w_all.shape[1]
    return pl.pallas_call(
        _memkv_kernel,
        out_shape=jax.ShapeDtypeStruct((b, m, n), BF16),
        grid=(b,),
        in_specs=[pl.BlockSpec((1, m, d), lambda i: (i, 0, 0)), _const_spec(w_all.shape)],
        out_specs=pl.BlockSpec((1, m, n), lambda i: (i, 0, 0)),
        compiler_params=_cparams(("arbitrary",)),
        name="mem_kv",
    )(mem, w_all)


def _decay_placement():
    pq = np.zeros((N_SPLIT * LANES, MIX_WIDTH), np.float32)
    pk = np.zeros((N_SPLIT * LANES, MIX_WIDTH), np.float32)
    cq = np.zeros((1, MIX_WIDTH), np.float32)
    ck = np.zeros((1, MIX_WIDTH), np.float32)
    for j in range(N_PAIRS):
        base = j * LANES
        ck[0, base:base + 2 * N_SPLIT] = 1.0
        cq[0, base + 2 * N_SPLIT:base + 4 * N_SPLIT] = -1.0
        for hh in range(2):
            for c in range(N_SPLIT):
                pq[c * LANES + 2 * j + hh, base + hh * N_SPLIT + c] = 1.0
                pk[c * LANES + 2 * j + hh, base + 2 * N_SPLIT + hh * N_SPLIT + c] = 1.0
    return pq, pk, cq, ck


def _fox_ext_lanes(hh):
    return tuple(range(hh * N_SPLIT, (hh + 1) * N_SPLIT)) + tuple(
        range(2 * N_SPLIT + hh * N_SPLIT, 2 * N_SPLIT + (hh + 1) * N_SPLIT))


def _fox_proj_kernel(x, w_ref, wvt_ref, bf_ref, pq_ref, pk_ref, cq_ref, ck_ref,
                     q_ref, k_ref, vt_ref, qx_ref, kx_ref, mq_ref, carry_ref, *, tm):
    si = pl.program_id(1)
    xb = x.astype(BF16)
    h = jnp.dot(xb, w_ref[...], preferred_element_type=F32)
    mw = MIX_WIDTH
    q_ref[0] = h[:, :mw].astype(BF16)
    k_ref[0] = h[:, mw:2 * mw].astype(BF16)
    mq_ref[0] = h[:, 2 * mw + LANES:].astype(BF16)
    vt_ref[0] = _dot_nt(wvt_ref[...], xb).astype(BF16)

    lf = jax.nn.log_sigmoid(h[:, 2 * mw:2 * mw + LANES] + bf_ref[...])
    row = lax.broadcasted_iota(jnp.int32, (tm, LANES), 0)
    shift = 1
    while shift < tm:
        lf = lf + jnp.where(row >= shift, pltpu.roll(lf, shift, 0), 0.0)
        shift *= 2

    @pl.when(si == 0)
    def _():
        carry_ref[...] = jnp.zeros_like(carry_ref)

    cum = lf + carry_ref[0:1, :]
    carry_ref[...] = jnp.broadcast_to(cum[tm - 1:tm, :], carry_ref.shape)

    parts = []
    rem = cum * LOG2E
    for _ in range(N_SPLIT):
        part = rem.astype(BF16)
        parts.append(part)
        rem = rem - part.astype(F32)
    split = jnp.concatenate(parts, axis=1)
    qx_ref[0] = (jnp.dot(split, pq_ref[...], preferred_element_type=F32)
                 + cq_ref[...]).astype(BF16)
    kx_ref[0] = (jnp.dot(split, pk_ref[...], preferred_element_type=F32)
                 + ck_ref[...]).astype(BF16)


def _ffn_fox_proj(x, wts, layer, alpha, jj, w_all, wvt_all, bf_all):
    b, s, d = x.shape
    tm = FFN_TM
    pq, pk, cq, ck = _decay_placement()
    pq = jnp.asarray(pq, BF16)
    pk = jnp.asarray(pk, BF16)
    cq = jnp.asarray(cq)
    ck = jnp.asarray(ck)
    tok = lambda w_: pl.BlockSpec((1, tm, w_), lambda bi, si: (bi, si, 0))
    sd = lambda w_: jax.ShapeDtypeStruct((b, s, w_), BF16)
    stacked = (w_all, wvt_all, bf_all)
    consts = (pq, pk, cq, ck)
    return _ffn_proj_call(
        "ffn_fox_proj", functools.partial(_fox_proj_kernel, tm=tm), x, wts, layer, alpha,
        stacked + consts,
        [_layer_spec(a, jj) for a in stacked] + [_const_spec(a.shape) for a in consts],
        (sd(MIX_WIDTH), sd(MIX_WIDTH), jax.ShapeDtypeStruct((b, MIX_WIDTH, s), BF16),
         sd(MIX_WIDTH), sd(MIX_WIDTH), sd(MEM_WIDTH)),
        (tok(MIX_WIDTH), tok(MIX_WIDTH),
         pl.BlockSpec((1, MIX_WIDTH, tm), lambda bi, si: (bi, 0, si)),
         tok(MIX_WIDTH), tok(MIX_WIDTH), tok(MEM_WIDTH)),
        scratch=[pltpu.VMEM((8, LANES), F32)])


def _pool_kernel(x, w_ref, wg_ref, sc_ref, main_ref, mq_ref, carry_ref, *, tm):
    si = pl.program_id(1)
    halo = carry_ref.shape[0]
    xb = x.astype(BF16)
    h = jnp.dot(xb, w_ref[...], preferred_element_type=F32)
    u = h[:, :MIX_WIDTH]
    mq_ref[0] = h[:, MIX_WIDTH:].astype(BF16)

    @pl.when(si == 0)
    def _():
        carry_ref[...] = jnp.zeros_like(carry_ref)

    ext = jnp.concatenate([carry_ref[...], u], axis=0)
    carry_ref[...] = u[tm - halo:, :]

    lane = lax.broadcasted_iota(jnp.int32, (1, MIX_WIDTH), 1)
    t_pos = si * tm + lax.broadcasted_iota(jnp.int32, (tm, 1), 0)
    acc = ext
    pooled = None
    span = 1
    for gi, win in enumerate(POOL_WINDOWS):
        while span < win:
            acc = acc + pltpu.roll(acc, span, 0)
            span *= 2
        cnt = jnp.minimum(t_pos + 1, win).astype(F32)
        mean = acc[halo:, :] / cnt
        if pooled is None:
            pooled = mean
        else:
            pooled = jnp.where(lane >= gi * POOL_GROUP_DIM, mean, pooled)
    pooled = pooled - u
    y = jnp.dot(pooled.astype(BF16), wg_ref[...], preferred_element_type=F32)
    main_ref[0] = (y * sc_ref[...]).astype(BF16)


def _ffn_pool_proj(x, wts, layer, alpha, w, w_bd, scale):
    b, s, d = x.shape
    tm = FFN_TM
    halo = 16
    tok = lambda w_: pl.BlockSpec((1, tm, w_), lambda bi, si: (bi, si, 0))
    args = (w, w_bd, scale)
    return _ffn_proj_call(
        "ffn_pool_proj", functools.partial(_pool_kernel, tm=tm), x, wts, layer, alpha, args,
        [_const_spec(a.shape) for a in args],
        (jax.ShapeDtypeStruct((b, s, MIX_WIDTH), BF16), jax.ShapeDtypeStruct((b, s, MEM_WIDTH), BF16)),
        (tok(MIX_WIDTH), tok(MEM_WIDTH)),
        scratch=[pltpu.VMEM((halo, MIX_WIDTH), F32)])


def _rope(xv, cos_t, sin_t, lane):
    half = QK_ROPE_DIM // 2
    swapped = jnp.where((lane % QK_ROPE_DIM) < half,
                        pltpu.roll(xv, LANES - half, 1), pltpu.roll(xv, half, 1))
    return xv * cos_t + swapped * sin_t


def _mla_proj_kernel(x, w_ref, qn_ref, kvn_ref, wuq_ref, wuk_ref, wuvt_ref, cos_ref, sin_ref,
                     qnope_ref, qpe_ref, knope_ref, kpe_ref, vt_ref, mq_ref, *, scale):
    xb = x.astype(BF16)
    h = jnp.dot(xb, w_ref[...], preferred_element_type=F32)
    o1 = Q_LORA_RANK
    o2 = o1 + KV_LORA_RANK
    o3 = o2 + LANES
    cq = _rms_norm(h[:, :o1], qn_ref[...]).astype(BF16)
    ckv = _rms_norm(h[:, o1:o2], kvn_ref[...]).astype(BF16)
    mq_ref[0] = h[:, o3:].astype(BF16)
    cos_t = cos_ref[0]
    sin_t = sin_ref[0]
    lane = lax.broadcasted_iota(jnp.int32, (1, LANES), 1)
    kpe_ref[0] = _rope(h[:, o2:o3], cos_t, sin_t, lane).astype(BF16)

    q = jnp.dot(cq, wuq_ref[...], preferred_element_type=F32) * scale
    qnope_ref[0] = q[:, :MIX_WIDTH].astype(BF16)
    for blk in range(N_PAIRS):
        lo = MIX_WIDTH + blk * LANES
        qpe_ref[0, :, blk * LANES:(blk + 1) * LANES] = _rope(
            q[:, lo:lo + LANES], cos_t, sin_t, lane).astype(BF16)
    knope_ref[0] = jnp.dot(ckv, wuk_ref[...], preferred_element_type=F32).astype(BF16)
    vt_ref[0] = _dot_nt(wuvt_ref[...], ckv).astype(BF16)


def _ffn_mla_proj(x, wts, layer, alpha, w, qn, kvn, wuq, wuk, wuvt, cos_t, sin_t, scale):
    b, s, d = x.shape
    tm = FFN_TM
    tok = lambda w_: pl.BlockSpec((1, tm, w_), lambda bi, si: (bi, si, 0))
    sd = lambda w_: jax.ShapeDtypeStruct((b, s, w_), BF16)
    consts = (w, qn, kvn, wuq, wuk, wuvt)
    return _ffn_proj_call(
        "ffn_mla_proj", functools.partial(_mla_proj_kernel, scale=scale), x, wts, layer, alpha,
        consts + (cos_t, sin_t),
        [_const_spec(a.shape) for a in consts] + [tok(LANES), tok(LANES)],
        (sd(MIX_WIDTH), sd(MIX_WIDTH), sd(MIX_WIDTH), sd(LANES),
         jax.ShapeDtypeStruct((b, MIX_WIDTH, s), BF16), sd(MEM_WIDTH)),
        (tok(MIX_WIDTH), tok(MIX_WIDTH), tok(MIX_WIDTH), tok(LANES),
         pl.BlockSpec((1, MIX_WIDTH, tm), lambda bi, si: (bi, 0, si)), tok(MEM_WIDTH)))


def _flash_kernel(qa_ref, qb_ref, ka_ref, kb_ref, vt_ref, o_ref, se_ref, so_ref, me_ref, mo_ref,
                  m_ref, acc_ref, *, t, pairs, kb_shared, b_lanes):
    g = pl.program_id(2)
    lane = lax.broadcasted_iota(jnp.int32, (1, LANES), 1)
    n_heads = 2 * pairs
    chain = lambda half, idx: half * n_heads + idx
    both = [(half, idx) for idx in range(n_heads) for half in range(2)]
    second = [(1, idx) for idx in range(n_heads)]

    qx = {}
    for half in range(2):
        rows = slice(half * t, (half + 1) * t)
        for p in range(pairs):
            qa = qa_ref[0, rows, p * LANES:(p + 1) * LANES]
            qb = qb_ref[0, rows, p * LANES:(p + 1) * LANES]
            for hh in range(2):
                bmask = functools.reduce(jnp.logical_or, [lane == ln for ln in b_lanes[hh]]) \
                    if len(b_lanes[hh]) <= 8 else (
                        (lane >= b_lanes[hh][0]) & (lane <= b_lanes[hh][-1]))
                qx[(half, 2 * p + hh)] = jnp.concatenate(
                    [jnp.where((lane // HEAD_DIM) == hh, qa, jnp.zeros_like(qa)),
                     jnp.where(bmask, qb, jnp.zeros_like(qb))], axis=1)

    for c in range(2 * n_heads):
        m_ref[c] = jnp.full((1, t), NEG_BIG, F32)
        acc_ref[c] = jnp.zeros(acc_ref.shape[1:], F32)
    ones_rows = jnp.ones((ACC_ROWS - HEAD_DIM, t), BF16)

    def key_block(kb, idx):
        ks = pl.multiple_of(kb * t, t)
        p = idx // 2
        kb_lo = 0 if kb_shared else p * LANES
        return jnp.concatenate([ka_ref[0, pl.ds(ks, t), p * LANES:(p + 1) * LANES],
                                kb_ref[0, pl.ds(ks, t), kb_lo:kb_lo + LANES]], axis=1)

    def score_chain(kb, bufs, half, idx):
        s_ref, mx_ref = bufs
        st = _dot_nt(key_block(kb, idx), qx[(half, idx)])
        s_ref[chain(half, idx)] = st
        mx_ref[chain(half, idx)] = jnp.max(st, axis=0, keepdims=True)

    def softmax_pv_chain(kb, bufs, half, idx, masked):
        s_ref, mx_ref = bufs
        c = chain(half, idx)
        ks = pl.multiple_of(kb * t, t)
        st = s_ref[c]
        if masked:
            kr = lax.broadcasted_iota(jnp.int32, (t, t), 0)
            qc = lax.broadcasted_iota(jnp.int32, (t, t), 1)
            st = jnp.where(kr <= qc, st, -jnp.inf)
            blk_max = jnp.max(st, axis=0, keepdims=True)
        else:
            blk_max = mx_ref[c]
        m_old = m_ref[c]
        m_new = jnp.maximum(m_old, blk_max)
        a = jnp.exp2(m_old - m_new)
        pt = jnp.exp2(st - m_new).astype(BF16)
        m_ref[c] = m_new
        vt = jnp.concatenate([vt_ref[0, idx * HEAD_DIM:(idx + 1) * HEAD_DIM, pl.ds(ks, t)],
                              ones_rows], axis=0)
        acc_ref[c] = a * acc_ref[c] + jnp.dot(vt, pt, preferred_element_type=F32)

    def step(kb, bufs, chains, masked_half=None, kb_next=None, bufs_next=None, next_chains=()):
        for half, idx in chains:
            if (half, idx) in next_chains:
                score_chain(kb_next, bufs_next, half, idx)
            softmax_pv_chain(kb, bufs, half, idx, masked_half == half)

    even = (se_ref, me_ref)
    odd = (so_ref, mo_ref)
    for half, idx in both:
        score_chain(0, even, half, idx)

    def body(i, carry):
        step(2 * i, even, both, None, 2 * i + 1, odd, both)
        step(2 * i + 1, odd, both, None, 2 * i + 2, even, both)
        return carry

    lax.fori_loop(0, g, body, 0)
    step(2 * g, even, both, 0, 2 * g + 1, odd, second)
    step(2 * g + 1, odd, second, 1)

    def normalised(c):
        acc = acc_ref[c]
        return acc[:HEAD_DIM, :] * (1.0 / acc[HEAD_DIM:HEAD_DIM + 1, :])

    for half in range(2):
        for p in range(pairs):
            ot = jnp.concatenate([normalised(chain(half, 2 * p)),
                                  normalised(chain(half, 2 * p + 1))], axis=0)
            o_ref[0, half * t:(half + 1) * t, p * LANES:(p + 1) * LANES] = ot.T.astype(BF16)


def _flash(qa, qb, ka, kb, vt, *, kb_shared, b_lanes):
    b, s, _ = qa.shape
    t = ATT_T
    pairs = ATT_PAIRS
    gw = pairs * LANES
    qspec = pl.BlockSpec((1, 2 * t, gw), lambda bi, g, qi: (bi, qi, g))
    kspec = pl.BlockSpec((1, s, gw), lambda bi, g, qi: (bi, 0, g))
    kbspec = pl.BlockSpec((1, s, LANES), lambda bi, g, qi: (bi, 0, 0)) if kb_shared else kspec
    vspec = pl.BlockSpec((1, gw, s), lambda bi, g, qi: (bi, g, 0))
    nc = 4 * pairs
    return pl.pallas_call(
        functools.partial(_flash_kernel, t=t, pairs=pairs, kb_shared=kb_shared, b_lanes=b_lanes),
        out_shape=jax.ShapeDtypeStruct((b, s, MIX_WIDTH), BF16),
        grid=(b, N_PAIRS // pairs, s // (2 * t)),
        in_specs=[qspec, qspec, kspec, kbspec, vspec],
        out_specs=qspec,
        scratch_shapes=[pltpu.VMEM((nc, t, t), F32), pltpu.VMEM((nc, t, t), F32),
                        pltpu.VMEM((nc, 1, t), F32), pltpu.VMEM((nc, 1, t), F32),
                        pltpu.VMEM((nc, 1, t), F32), pltpu.VMEM((nc, ACC_ROWS, t), F32)],
        compiler_params=_cparams(("arbitrary", "arbitrary", "arbitrary")),
        name="flash",
    )(qa, qb, ka, kb, vt)


def _mix_body(main_ref, mq_ref, mk_ref, mv_ref, wo_ref, x, g_ref, b_ref, alpha):
    lane = lax.broadcasted_iota(jnp.int32, (1, LANES), 1)
    first = (lane // HEAD_DIM) == 0
    mix = jnp.dot(main_ref[0], wo_ref[:MIX_WIDTH, :], preferred_element_type=F32)
    for pr in range(MEM_WIDTH // LANES):
        sl = slice(pr * LANES, (pr + 1) * LANES)
        q2 = mq_ref[0, :, sl]
        k2 = mk_ref[0, :, sl]
        v2 = mv_ref[0, :, sl]
        outs = []
        for hh in range(2):
            qm = jnp.where((lane // HEAD_DIM) == hh, q2, jnp.zeros_like(q2))
            s = _dot_nt(qm, k2)
            m = jnp.max(s, axis=-1, keepdims=True)
            p = jnp.exp(s - m)
            l = jnp.sum(p, axis=-1, keepdims=True)
            outs.append(jnp.dot(p.astype(BF16), v2, preferred_element_type=F32) / l)
        mo = jnp.where(first, outs[0], outs[1]).astype(BF16)
        mix = mix + jnp.dot(mo, wo_ref[MIX_WIDTH + pr * LANES:MIX_WIDTH + (pr + 1) * LANES, :],
                            preferred_element_type=F32)
    return _layer_norm(alpha * x + mix, g_ref[...], b_ref[...])


def _mix_ffn_kernel(main_ref, mq_ref, mk_ref, mv_ref, wo_ref, x_ref, g1_ref, b1_ref,
                    wg_ref, wu_ref, wd_ref, g2_ref, b2_ref, o_ref, *, alpha):
    x1 = _mix_body(main_ref, mq_ref, mk_ref, mv_ref, wo_ref, x_ref[0], g1_ref, b1_ref, alpha)
    o_ref[0] = _ffn_body(x1, wg_ref, wu_ref, wd_ref, g2_ref, b2_ref, alpha)


def _mix_ffn(main, mq, memkv, layer, wo, x, wts, alpha):
    bsz, s, d = x.shape
    tm = FFN_TM
    m = memkv.shape[1]
    ln_g, ln_b = wts[3], wts[4]
    tok = lambda w_: pl.BlockSpec((1, tm, w_), lambda bi, si: (bi, si, 0))
    specs, args = _ffn_specs(wts, layer, 1, 2)
    return pl.pallas_call(
        functools.partial(_mix_ffn_kernel, alpha=alpha),
        out_shape=jax.ShapeDtypeStruct((bsz, s, d), F32),
        grid=(bsz, s // tm),
        in_specs=[tok(MIX_WIDTH), tok(MEM_WIDTH),
                  pl.BlockSpec((1, m, MEM_WIDTH), lambda bi, si: (bi, 0, 2 * layer)),
                  pl.BlockSpec((1, m, MEM_WIDTH), lambda bi, si: (bi, 0, 2 * layer + 1)),
                  _layer_spec(wo, layer), tok(d), _layer_spec(ln_g, layer, 1),
                  _layer_spec(ln_b, layer, 1)] + specs,
        out_specs=tok(d),
        compiler_params=_cparams(("arbitrary", "arbitrary")),
        name="mix_ffn",
    )(main, mq, memkv, memkv, wo, x, ln_g, ln_b, *args)


def _pair_pad_cols(w3, lo, width):
    k = w3.shape[0]
    part = w3[:, :, lo:lo + width].reshape(k, N_PAIRS, 2 * width)
    return jnp.pad(part, ((0, 0), (0, 0), (0, LANES - 2 * width))).reshape(k, N_PAIRS * LANES)


def kernel(x, mem, positions, ln_g, ln_b, ffn_w_gate, ffn_w_up, ffn_w_down, mem_w_kv, w_o,
           fox_w_in, fox_b_f, pool_w_in, pool_w_grp, pool_scale, mla_w_in, mla_q_norm,
           mla_kv_norm, mla_w_uq, mla_w_ukv):
    bsz, s, d = x.shape
    depth = ln_g.shape[0]
    alpha = float((2 * depth) ** 0.25)
    att_scale = HEAD_DIM ** -0.5
    mw = MIX_WIDTH

    memkv = _mem_kv(mem, jnp.concatenate([mem_w_kv[i] for i in range(depth)], axis=1).astype(BF16))

    inv_freq = ROPE_THETA ** (-jnp.arange(0, QK_ROPE_DIM, 2, dtype=F32) / QK_ROPE_DIM)
    half = QK_ROPE_DIM // 2
    lane_id = np.arange(LANES)
    live = jnp.asarray(lane_id < 2 * QK_ROPE_DIM)
    sign = jnp.asarray(np.where((lane_id % QK_ROPE_DIM) < half, -1.0, 1.0).astype(np.float32))
    ang = positions.astype(F32)[..., None] * jnp.tile(inv_freq, LANES // half)
    cos_t = jnp.where(live, jnp.cos(ang), 0.0)
    sin_t = jnp.where(live, sign * jnp.sin(ang), 0.0)

    wts = (ffn_w_gate.astype(BF16), ffn_w_up.astype(BF16), ffn_w_down.astype(BF16),
           ln_g[:, :, None, :], ln_b[:, :, None, :])
    wo_all = w_o.astype(BF16)

    gate_pad = ((0, 0), (0, 0), (0, LANES - N_MIX_HEADS))
    fox_w = jnp.concatenate([
        fox_w_in[:, :, :mw] * (att_scale * LOG2E), fox_w_in[:, :, mw:2 * mw],
        jnp.pad(fox_w_in[:, :, 3 * mw:3 * mw + N_MIX_HEADS], gate_pad),
        fox_w_in[:, :, 3 * mw + N_MIX_HEADS:] * att_scale], axis=2).astype(BF16)
    fox_wvt = jnp.swapaxes(fox_w_in[:, :, 2 * mw:3 * mw], 1, 2).astype(BF16)
    fox_bf = jnp.pad(fox_b_f, ((0, 0), (0, LANES - N_MIX_HEADS)))[:, None, :]

    for i in range(depth):
        kind, jj = i % N_MIXERS, i // N_MIXERS
        if kind == 0:
            x, q, k, vt, qx, kx, mq = _ffn_fox_proj(x, wts, i, alpha, jj, fox_w, fox_wvt, fox_bf)
            main = _flash(q, qx, k, kx, vt, kb_shared=False,
                          b_lanes=(_fox_ext_lanes(0), _fox_ext_lanes(1)))
        elif kind == 1:
            w = pool_w_in[jj]
            w = jnp.concatenate([w[:, :mw], w[:, mw:] * att_scale], axis=1).astype(BF16)
            w_bd = jax.scipy.linalg.block_diag(*[pool_w_grp[jj, gidx]
                                                 for gidx in range(len(POOL_WINDOWS))]).astype(BF16)
            x, main, mq = _ffn_pool_proj(x, wts, i, alpha, w, w_bd, pool_scale[jj][None])
        else:
            w = mla_w_in[jj]
            o2 = Q_LORA_RANK + KV_LORA_RANK
            kr = w[:, o2:o2 + QK_ROPE_DIM]
            w = jnp.concatenate([w[:, :o2], kr, kr,
                                 jnp.zeros((d, LANES - 2 * QK_ROPE_DIM), F32),
                                 w[:, o2 + QK_ROPE_DIM:] * att_scale], axis=1).astype(BF16)
            qd = QK_NOPE_DIM + QK_ROPE_DIM
            wuq3 = mla_w_uq[jj].reshape(Q_LORA_RANK, N_MIX_HEADS, qd)
            wuq = jnp.concatenate([wuq3[:, :, :QK_NOPE_DIM].reshape(Q_LORA_RANK, mw),
                                   _pair_pad_cols(wuq3, QK_NOPE_DIM, QK_ROPE_DIM)],
                                  axis=1).astype(BF16)
            kvd = QK_NOPE_DIM + V_HEAD_DIM
            wukv3 = mla_w_ukv[jj].reshape(KV_LORA_RANK, N_MIX_HEADS, kvd)
            wuk = wukv3[:, :, :QK_NOPE_DIM].reshape(KV_LORA_RANK, mw).astype(BF16)
            wuvt = wukv3[:, :, QK_NOPE_DIM:].reshape(KV_LORA_RANK, mw).T.astype(BF16)
            x, qn, qpe, kn, kpe, vt, mq = _ffn_mla_proj(
                x, wts, i, alpha, w, mla_q_norm[jj][None], mla_kv_norm[jj][None], wuq, wuk, wuvt,
                cos_t, sin_t, float(qd ** -0.5) * LOG2E)
            rope_lanes = (tuple(range(QK_ROPE_DIM)), tuple(range(QK_ROPE_DIM, 2 * QK_ROPE_DIM)))
            main = _flash(qn, qpe, kn, kpe, vt, kb_shared=True, b_lanes=rope_lanes)
        x = _mix_ffn(main, mq, memkv, i, wo_all, x, wts, alpha)
    return x
```

```python
import functools
import math

import numpy as np
import jax
import jax.numpy as jnp
from jax import lax
from jax.experimental import pallas as pl
from jax.experimental.pallas import tpu as pltpu

F32 = jnp.float32
BF16 = jnp.bfloat16

N_MIXERS = 3
HEAD_DIM = 64
MIX_WIDTH = 768
N_MIX_HEADS = MIX_WIDTH // HEAD_DIM
N_MEM_HEADS = 4
MEM_WIDTH = N_MEM_HEADS * HEAD_DIM
POOL_WINDOWS = (2, 4, 8, 16)
POOL_GROUP_DIM = MIX_WIDTH // len(POOL_WINDOWS)
Q_LORA_RANK = 384
KV_LORA_RANK = 256
QK_NOPE_DIM = 64
QK_ROPE_DIM = 32
V_HEAD_DIM = 64
ROPE_THETA = 10000.0
LN_EPS = 1e-5
RMS_EPS = 1e-6

LANES = 128
VMEM_LIMIT_BYTES = 56 * 1024 * 1024

FFN_TM = 512
ATT_T = 256
ATT_PAIRS = 6
N_PAIRS = N_MIX_HEADS // 2
ACC_ROWS = HEAD_DIM + 16
NEG_BIG = -1e30
LOG2E = math.log2(math.e)
N_SPLIT = 3


def _cparams(sem):
    return pltpu.CompilerParams(dimension_semantics=sem, vmem_limit_bytes=VMEM_LIMIT_BYTES)


def _const_spec(shape):
    nd = len(shape)
    return pl.BlockSpec(shape, lambda *_: (0,) * nd, pipeline_mode=pl.Buffered(1))


def _layer_norm(y, g, b):
    mu = jnp.mean(y, axis=-1, keepdims=True)
    d = y - mu
    var = jnp.mean(d * d, axis=-1, keepdims=True)
    return d * lax.rsqrt(var + LN_EPS) * g + b


def _rms_norm(y, g):
    ms = jnp.mean(y * y, axis=-1, keepdims=True)
    return y * lax.rsqrt(ms + RMS_EPS) * g


def _dot_nt(a, b):
    return lax.dot_general(a, b, (((1,), (1,)), ((), ())), preferred_element_type=F32)


def _ffn_body(x, wg_ref, wu_ref, wd_ref, g_ref, b_ref, alpha):
    xb = x.astype(BF16)
    gate = jnp.dot(xb, wg_ref[...], preferred_element_type=F32)
    up = jnp.dot(xb, wu_ref[...], preferred_element_type=F32)
    h = ((gate * jax.nn.sigmoid(gate)) * up).astype(BF16)
    ff = jnp.dot(h, wd_ref[...], preferred_element_type=F32)
    return _layer_norm(alpha * x + 0.5 * ff, g_ref[...], b_ref[...])


def _layer_spec(arr, *idx):
    lead = len(idx)
    shape = (None,) * lead + tuple(arr.shape[lead:])
    tail = (0,) * (arr.ndim - lead)
    return pl.BlockSpec(shape, lambda *_: tuple(idx) + tail, pipeline_mode=pl.Buffered(1))


def _ffn_specs(wts, layer, half, ln_idx):
    wg, wu, wd, ln_g, ln_b = wts
    return ([_layer_spec(wg, layer, half), _layer_spec(wu, layer, half),
             _layer_spec(wd, layer, half), _layer_spec(ln_g, layer, ln_idx),
             _layer_spec(ln_b, layer, ln_idx)], [wg, wu, wd, ln_g, ln_b])


def _ffn_proj_call(name, proj_kernel, x, wts, layer, alpha, proj_args, proj_specs, out_shape,
                   out_specs, scratch=()):
    bsz, s, d = x.shape
    tm = FFN_TM
    tok = pl.BlockSpec((1, tm, d), lambda bi, si: (bi, si, 0))
    fspecs, fargs = _ffn_specs(wts, layer, 0, 0)
    n_in = len(proj_args)

    def fused(x_ref, wg_ref, wu_ref, wd_ref, g_ref, b_ref, *rest):
        x1 = _ffn_body(x_ref[0], wg_ref, wu_ref, wd_ref, g_ref, b_ref, alpha)
        rest[n_in][0] = x1
        proj_kernel(x1, *rest[:n_in], *rest[n_in + 1:])

    return pl.pallas_call(
        fused,
        out_shape=(jax.ShapeDtypeStruct((bsz, s, d), F32),) + tuple(out_shape),
        grid=(bsz, s // tm),
        in_specs=[tok] + fspecs + list(proj_specs),
        out_specs=(tok,) + tuple(out_specs),
        scratch_shapes=list(scratch),
        compiler_params=_cparams(("arbitrary", "arbitrary")),
        name=name,
    )(x, *fargs, *proj_args)


def _memkv_kernel(m_ref, w_ref, o_ref):
    o_ref[0] = jnp.dot(m_ref[0].astype(BF16), w_ref[...], preferred_element_type=F32).astype(BF16)


def _mem_kv(mem, w_all):
    b, m, d = mem.shape
    n = w_all.shape[1]
    return pl.pallas_call(
        _memkv_kernel,
        out_shape=jax.ShapeDtypeStruct((b, m, n), BF16),
        grid=(b,),
        in_specs=[pl.BlockSpec((1, m, d), lambda i: (i, 0, 0)), _const_spec(w_all.shape)],
        out_specs=pl.BlockSpec((1, m, n), lambda i: (i, 0, 0)),
        compiler_params=_cparams(("arbitrary",)),
        name="mem_kv",
    )(mem, w_all)


def _decay_placement():
    pq = np.zeros((N_SPLIT * LANES, MIX_WIDTH), np.float32)
    pk = np.zeros((N_SPLIT * LANES, MIX_WIDTH), np.float32)
    cq = np.zeros((1, MIX_WIDTH), np.float32)
    ck = np.zeros((1, MIX_WIDTH), np.float32)
    for j in range(N_PAIRS):
        base = j * LANES
        ck[0, base:base + 2 * N_SPLIT] = 1.0
        cq[0, base + 2 * N_SPLIT:base + 4 * N_SPLIT] = -1.0
        for hh in range(2):
            for c in range(N_SPLIT):
                pq[c * LANES + 2 * j + hh, base + hh * N_SPLIT + c] = 1.0
                pk[c * LANES + 2 * j + hh, base + 2 * N_SPLIT + hh * N_SPLIT + c] = 1.0
    return pq, pk, cq, ck


def _fox_ext_lanes(hh):
    return tuple(range(hh * N_SPLIT, (hh + 1) * N_SPLIT)) + tuple(
        range(2 * N_SPLIT + hh * N_SPLIT, 2 * N_SPLIT + (hh + 1) * N_SPLIT))


def _fox_proj_kernel(x, w_ref, wvt_ref, bf_ref, pq_ref, pk_ref, cq_ref, ck_ref,
                     q_ref, k_ref, vt_ref, qx_ref, kx_ref, mq_ref, carry_ref, *, tm):
    si = pl.program_id(1)
    xb = x.astype(BF16)
    h = jnp.dot(xb, w_ref[...], preferred_element_type=F32)
    mw = MIX_WIDTH
    q_ref[0] = h[:, :mw].astype(BF16)
    k_ref[0] = h[:, mw:2 * mw].astype(BF16)
    mq_ref[0] = h[:, 2 * mw + LANES:].astype(BF16)
    vt_ref[0] = _dot_nt(wvt_ref[...], xb).astype(BF16)

    lf = jax.nn.log_sigmoid(h[:, 2 * mw:2 * mw + LANES] + bf_ref[...])
    row = lax.broadcasted_iota(jnp.int32, (tm, LANES), 0)
    shift = 1
    while shift < tm:
        lf = lf + jnp.where(row >= shift, pltpu.roll(lf, shift, 0), 0.0)
        shift *= 2

    @pl.when(si == 0)
    def _():
        carry_ref[...] = jnp.zeros_like(carry_ref)

    cum = lf + carry_ref[0:1, :]
    carry_ref[...] = jnp.broadcast_to(cum[tm - 1:tm, :], carry_ref.shape)

    parts = []
    rem = cum * LOG2E
    for _ in range(N_SPLIT):
        part = rem.astype(BF16)
        parts.append(part)
        rem = rem - part.astype(F32)
    split = jnp.concatenate(parts, axis=1)
    qx_ref[0] = (jnp.dot(split, pq_ref[...], preferred_element_type=F32)
                 + cq_ref[...]).astype(BF16)
    kx_ref[0] = (jnp.dot(split, pk_ref[...], preferred_element_type=F32)
                 + ck_ref[...]).astype(BF16)


def _ffn_fox_proj(x, wts, layer, alpha, jj, w_all, wvt_all, bf_all):
    b, s, d = x.shape
    tm = FFN_TM
    pq, pk, cq, ck = _decay_placement()
    pq = jnp.asarray(pq, BF16)
    pk = jnp.asarray(pk, BF16)
    cq = jnp.asarray(cq)
    ck = jnp.asarray(ck)
    tok = lambda w_: pl.BlockSpec((1, tm, w_), lambda bi, si: (bi, si, 0))
    sd = lambda w_: jax.ShapeDtypeStruct((b, s, w_), BF16)
    stacked = (w_all, wvt_all, bf_all)
    consts = (pq, pk, cq, ck)
    return _ffn_proj_call(
        "ffn_fox_proj", functools.partial(_fox_proj_kernel, tm=tm), x, wts, layer, alpha,
        stacked + consts,
        [_layer_spec(a, jj) for a in stacked] + [_const_spec(a.shape) for a in consts],
        (sd(MIX_WIDTH), sd(MIX_WIDTH), jax.ShapeDtypeStruct((b, MIX_WIDTH, s), BF16),
         sd(MIX_WIDTH), sd(MIX_WIDTH), sd(MEM_WIDTH)),
        (tok(MIX_WIDTH), tok(MIX_WIDTH),
         pl.BlockSpec((1, MIX_WIDTH, tm), lambda bi, si: (bi, 0, si)),
         tok(MIX_WIDTH), tok(MIX_WIDTH), tok(MEM_WIDTH)),
        scratch=[pltpu.VMEM((8, LANES), F32)])


def _pool_kernel(x, w_ref, wg_ref, sc_ref, main_ref, mq_ref, carry_ref, *, tm):
    si = pl.program_id(1)
    halo = carry_ref.shape[0]
    xb = x.astype(BF16)
    h = jnp.dot(xb, w_ref[...], preferred_element_type=F32)
    u = h[:, :MIX_WIDTH]
    mq_ref[0] = h[:, MIX_WIDTH:].astype(BF16)

    @pl.when(si == 0)
    def _():
        carry_ref[...] = jnp.zeros_like(carry_ref)

    ext = jnp.concatenate([carry_ref[...], u], axis=0)
    carry_ref[...] = u[tm - halo:, :]

    lane = lax.broadcasted_iota(jnp.int32, (1, MIX_WIDTH), 1)
    t_pos = si * tm + lax.broadcasted_iota(jnp.int32, (tm, 1), 0)
    acc = ext
    pooled = None
    span = 1
    for gi, win in enumerate(POOL_WINDOWS):
        while span < win:
            acc = acc + pltpu.roll(acc, span, 0)
            span *= 2
        cnt = jnp.minimum(t_pos + 1, win).astype(F32)
        mean = acc[halo:, :] / cnt
        if pooled is None:
            pooled = mean
        else:
            pooled = jnp.where(lane >= gi * POOL_GROUP_DIM, mean, pooled)
    pooled = pooled - u
    y = jnp.dot(pooled.astype(BF16), wg_ref[...], preferred_element_type=F32)
    main_ref[0] = (y * sc_ref[...]).astype(BF16)


def _ffn_pool_proj(x, wts, layer, alpha, w, w_bd, scale):
    b, s, d = x.shape
    tm = FFN_TM
    halo = 16
    tok = lambda w_: pl.BlockSpec((1, tm, w_), lambda bi, si: (bi, si, 0))
    args = (w, w_bd, scale)
    return _ffn_proj_call(
        "ffn_pool_proj", functools.partial(_pool_kernel, tm=tm), x, wts, layer, alpha, args,
        [_const_spec(a.shape) for a in args],
        (jax.ShapeDtypeStruct((b, s, MIX_WIDTH), BF16), jax.ShapeDtypeStruct((b, s, MEM_WIDTH), BF16)),
        (tok(MIX_WIDTH), tok(MEM_WIDTH)),
        scratch=[pltpu.VMEM((halo, MIX_WIDTH), F32)])


def _rope(xv, cos_t, sin_t, lane):
    half = QK_ROPE_DIM // 2
    swapped = jnp.where((lane % QK_ROPE_DIM) < half,
                        pltpu.roll(xv, LANES - half, 1), pltpu.roll(xv, half, 1))
    return xv * cos_t + swapped * sin_t


def _mla_proj_kernel(x, w_ref, qn_ref, kvn_ref, wuq_ref, wuk_ref, wuvt_ref, cos_ref, sin_ref,
                     qnope_ref, qpe_ref, knope_ref, kpe_ref, vt_ref, mq_ref, *, scale):
    xb = x.astype(BF16)
    h = jnp.dot(xb, w_ref[...], preferred_element_type=F32)
    o1 = Q_LORA_RANK
    o2 = o1 + KV_LORA_RANK
    o3 = o2 + LANES
    cq = _rms_norm(h[:, :o1], qn_ref[...]).astype(BF16)
    ckv = _rms_norm(h[:, o1:o2], kvn_ref[...]).astype(BF16)
    mq_ref[0] = h[:, o3:].astype(BF16)
    cos_t = cos_ref[0]
    sin_t = sin_ref[0]
    lane = lax.broadcasted_iota(jnp.int32, (1, LANES), 1)
    kpe_ref[0] = _rope(h[:, o2:o3], cos_t, sin_t, lane).astype(BF16)

    q = jnp.dot(cq, wuq_ref[...], preferred_element_type=F32) * scale
    qnope_ref[0] = q[:, :MIX_WIDTH].astype(BF16)
    for blk in range(N_PAIRS):
        lo = MIX_WIDTH + blk * LANES
        qpe_ref[0, :, blk * LANES:(blk + 1) * LANES] = _rope(
            q[:, lo:lo + LANES], cos_t, sin_t, lane).astype(BF16)
    knope_ref[0] = jnp.dot(ckv, wuk_ref[...], preferred_element_type=F32).astype(BF16)
    vt_ref[0] = _dot_nt(wuvt_ref[...], ckv).astype(BF16)


def _ffn_mla_proj(x, wts, layer, alpha, w, qn, kvn, wuq, wuk, wuvt, cos_t, sin_t, scale):
    b, s, d = x.shape
    tm = FFN_TM
    tok = lambda w_: pl.BlockSpec((1, tm, w_), lambda bi, si: (bi, si, 0))
    sd = lambda w_: jax.ShapeDtypeStruct((b, s, w_), BF16)
    consts = (w, qn, kvn, wuq, wuk, wuvt)
    return _ffn_proj_call(
        "ffn_mla_proj", functools.partial(_mla_proj_kernel, scale=scale), x, wts, layer, alpha,
        consts + (cos_t, sin_t),
        [_const_spec(a.shape) for a in consts] + [tok(LANES), tok(LANES)],
        (sd(MIX_WIDTH), sd(MIX_WIDTH), sd(MIX_WIDTH), sd(LANES),
         jax.ShapeDtypeStruct((b, MIX_WIDTH, s), BF16), sd(MEM_WIDTH)),
        (tok(MIX_WIDTH), tok(MIX_WIDTH), tok(MIX_WIDTH), tok(LANES),
         pl.BlockSpec((1, MIX_WIDTH, tm), lambda bi, si: (bi, 0, si)), tok(MEM_WIDTH)))


def _flash_kernel(qa_ref, qb_ref, ka_ref, kb_ref, vt_ref, o_ref, qx_ref, se_ref, so_ref, me_ref,
                  mo_ref, m_ref, acc_ref, *, t, pairs, kb_shared, b_lanes):
    g = pl.program_id(2)
    lane = lax.broadcasted_iota(jnp.int32, (1, LANES), 1)
    n_heads = 2 * pairs
    chain = lambda half, idx: half * n_heads + idx
    both = [(half, idx) for idx in range(n_heads) for half in range(2)]
    second = [(1, idx) for idx in range(n_heads)]

    for half in range(2):
        rows = slice(half * t, (half + 1) * t)
        for p in range(pairs):
            qa = qa_ref[0, rows, p * LANES:(p + 1) * LANES]
            qb = qb_ref[0, rows, p * LANES:(p + 1) * LANES]
            for hh in range(2):
                bmask = functools.reduce(jnp.logical_or, [lane == ln for ln in b_lanes[hh]]) \
                    if len(b_lanes[hh]) <= 8 else (
                        (lane >= b_lanes[hh][0]) & (lane <= b_lanes[hh][-1]))
                qx_ref[chain(half, 2 * p + hh)] = jnp.concatenate(
                    [jnp.where((lane // HEAD_DIM) == hh, qa, jnp.zeros_like(qa)),
                     jnp.where(bmask, qb, jnp.zeros_like(qb))], axis=1).T

    for c in range(2 * n_heads):
        m_ref[c] = jnp.full((1, t), NEG_BIG, F32)
        acc_ref[c] = jnp.zeros(acc_ref.shape[1:], F32)
    ones_rows = jnp.ones((ACC_ROWS - HEAD_DIM, t), BF16)

    def key_block(kb, idx):
        ks = pl.multiple_of(kb * t, t)
        p = idx // 2
        kb_lo = 0 if kb_shared else p * LANES
        return jnp.concatenate([ka_ref[0, pl.ds(ks, t), p * LANES:(p + 1) * LANES],
                                kb_ref[0, pl.ds(ks, t), kb_lo:kb_lo + LANES]], axis=1)

    def score_chain(kb, bufs, half, idx):
        s_ref, mx_ref = bufs
        st = jnp.dot(key_block(kb, idx), qx_ref[chain(half, idx)],
                     preferred_element_type=F32)
        s_ref[chain(half, idx)] = st
        mx_ref[chain(half, idx)] = jnp.max(st, axis=0, keepdims=True)

    def softmax_pv_chain(kb, bufs, half, idx, masked):
        s_ref, mx_ref = bufs
        c = chain(half, idx)
        ks = pl.multiple_of(kb * t, t)
        st = s_ref[c]
        if masked:
            kr = lax.broadcasted_iota(jnp.int32, (t, t), 0)
            qc = lax.broadcasted_iota(jnp.int32, (t, t), 1)
            st = jnp.where(kr <= qc, st, -jnp.inf)
            blk_max = jnp.max(st, axis=0, keepdims=True)
        else:
            blk_max = mx_ref[c]
        m_old = m_ref[c]
        m_new = jnp.maximum(m_old, blk_max)
        a = jnp.exp2(m_old - m_new)
        pt = jnp.exp2(st - m_new).astype(BF16)
        m_ref[c] = m_new
        vt = jnp.concatenate([vt_ref[0, idx * HEAD_DIM:(idx + 1) * HEAD_DIM, pl.ds(ks, t)],
                              ones_rows], axis=0)
        acc_ref[c] = a * acc_ref[c] + jnp.dot(vt, pt, preferred_element_type=F32)

    def step(kb, bufs, chains, masked_half=None, kb_next=None, bufs_next=None, next_chains=()):
        for half, idx in chains:
            if (half, idx) in next_chains:
                score_chain(kb_next, bufs_next, half, idx)
            softmax_pv_chain(kb, bufs, half, idx, masked_half == half)

    even = (se_ref, me_ref)
    odd = (so_ref, mo_ref)
    for half, idx in both:
        score_chain(0, even, half, idx)

    def body(i, carry):
        step(2 * i, even, both, None, 2 * i + 1, odd, both)
        step(2 * i + 1, odd, both, None, 2 * i + 2, even, both)
        return carry

    lax.fori_loop(0, g, body, 0)
    step(2 * g, even, both, 0, 2 * g + 1, odd, second)
    step(2 * g + 1, odd, second, 1)

    def normalised(c):
        acc = acc_ref[c]
        return acc[:HEAD_DIM, :] * (1.0 / acc[HEAD_DIM:HEAD_DIM + 1, :])

    for half in range(2):
        for p in range(pairs):
            ot = jnp.concatenate([normalised(chain(half, 2 * p)),
                                  normalised(chain(half, 2 * p + 1))], axis=0)
            o_ref[0, half * t:(half + 1) * t, p * LANES:(p + 1) * LANES] = ot.T.astype(BF16)


def _flash(qa, qb, ka, kb, vt, *, kb_shared, b_lanes):
    b, s, _ = qa.shape
    t = ATT_T
    pairs = ATT_PAIRS
    gw = pairs * LANES
    qspec = pl.BlockSpec((1, 2 * t, gw), lambda bi, g, qi: (bi, qi, g))
    kspec = pl.BlockSpec((1, s, gw), lambda bi, g, qi: (bi, 0, g))
    kbspec = pl.BlockSpec((1, s, LANES), lambda bi, g, qi: (bi, 0, 0)) if kb_shared else kspec
    vspec = pl.BlockSpec((1, gw, s), lambda bi, g, qi: (bi, g, 0))
    nc = 4 * pairs
    return pl.pallas_call(
        functools.partial(_flash_kernel, t=t, pairs=pairs, kb_shared=kb_shared, b_lanes=b_lanes),
        out_shape=jax.ShapeDtypeStruct((b, s, MIX_WIDTH), BF16),
        grid=(b, N_PAIRS // pairs, s // (2 * t)),
        in_specs=[qspec, qspec, kspec, kbspec, vspec],
        out_specs=qspec,
        scratch_shapes=[pltpu.VMEM((nc, 2 * LANES, t), BF16),
                        pltpu.VMEM((nc, t, t), F32), pltpu.VMEM((nc, t, t), F32),
                        pltpu.VMEM((nc, 1, t), F32), pltpu.VMEM((nc, 1, t), F32),
                        pltpu.VMEM((nc, 1, t), F32), pltpu.VMEM((nc, ACC_ROWS, t), F32)],
        compiler_params=_cparams(("arbitrary", "arbitrary", "arbitrary")),
        name="flash",
    )(qa, qb, ka, kb, vt)


def _mix_body(main_ref, mq_ref, mk_ref, mv_ref, wo_ref, x, g_ref, b_ref, alpha):
    lane = lax.broadcasted_iota(jnp.int32, (1, LANES), 1)
    first = (lane // HEAD_DIM) == 0
    mix = jnp.dot(main_ref[0], wo_ref[:MIX_WIDTH, :], preferred_element_type=F32)
    for pr in range(MEM_WIDTH // LANES):
        sl = slice(pr * LANES, (pr + 1) * LANES)
        q2 = mq_ref[0, :, sl]
        k2 = mk_ref[0, :, sl]
        v2 = mv_ref[0, :, sl]
        outs = []
        for hh in range(2):
            qm = jnp.where((lane // HEAD_DIM) == hh, q2, jnp.zeros_like(q2))
            s = _dot_nt(qm, k2)
            m = jnp.max(s, axis=-1, keepdims=True)
            p = jnp.exp(s - m)
            l = jnp.sum(p, axis=-1, keepdims=True)
            outs.append(jnp.dot(p.astype(BF16), v2, preferred_element_type=F32) / l)
        mo = jnp.where(first, outs[0], outs[1]).astype(BF16)
        mix = mix + jnp.dot(mo, wo_ref[MIX_WIDTH + pr * LANES:MIX_WIDTH + (pr + 1) * LANES, :],
                            preferred_element_type=F32)
    return _layer_norm(alpha * x + mix, g_ref[...], b_ref[...])


def _mix_ffn_kernel(main_ref, mq_ref, mk_ref, mv_ref, wo_ref, x_ref, g1_ref, b1_ref,
                    wg_ref, wu_ref, wd_ref, g2_ref, b2_ref, o_ref, *, alpha):
    x1 = _mix_body(main_ref, mq_ref, mk_ref, mv_ref, wo_ref, x_ref[0], g1_ref, b1_ref, alpha)
    o_ref[0] = _ffn_body(x1, wg_ref, wu_ref, wd_ref, g2_ref, b2_ref, alpha)


def _mix_ffn(main, mq, memkv, layer, wo, x, wts, alpha):
    bsz, s, d = x.shape
    tm = FFN_TM
    m = memkv.shape[1]
    ln_g, ln_b = wts[3], wts[4]
    tok = lambda w_: pl.BlockSpec((1, tm, w_), lambda bi, si: (bi, si, 0))
    specs, args = _ffn_specs(wts, layer, 1, 2)
    return pl.pallas_call(
        functools.partial(_mix_ffn_kernel, alpha=alpha),
        out_shape=jax.ShapeDtypeStruct((bsz, s, d), F32),
        grid=(bsz, s // tm),
        in_specs=[tok(MIX_WIDTH), tok(MEM_WIDTH),
                  pl.BlockSpec((1, m, MEM_WIDTH), lambda bi, si: (bi, 0, 2 * layer)),
                  pl.BlockSpec((1, m, MEM_WIDTH), lambda bi, si: (bi, 0, 2 * layer + 1)),
                  _layer_spec(wo, layer), tok(d), _layer_spec(ln_g, layer, 1),
                  _layer_spec(ln_b, layer, 1)] + specs,
        out_specs=tok(d),
        compiler_params=_cparams(("arbitrary", "arbitrary")),
        name="mix_ffn",
    )(main, mq, memkv, memkv, wo, x, ln_g, ln_b, *args)


def _pair_pad_cols(w3, lo, width):
    k = w3.shape[0]
    part = w3[:, :, lo:lo + width].reshape(k, N_PAIRS, 2 * width)
    return jnp.pad(part, ((0, 0), (0, 0), (0, LANES - 2 * width))).reshape(k, N_PAIRS * LANES)


def kernel(x, mem, positions, ln_g, ln_b, ffn_w_gate, ffn_w_up, ffn_w_down, mem_w_kv, w_o,
           fox_w_in, fox_b_f, pool_w_in, pool_w_grp, pool_scale, mla_w_in, mla_q_norm,
           mla_kv_norm, mla_w_uq, mla_w_ukv):
    bsz, s, d = x.shape
    depth = ln_g.shape[0]
    alpha = float((2 * depth) ** 0.25)
    att_scale = HEAD_DIM ** -0.5
    mw = MIX_WIDTH

    memkv = _mem_kv(mem, jnp.concatenate([mem_w_kv[i] for i in range(depth)], axis=1).astype(BF16))

    inv_freq = ROPE_THETA ** (-jnp.arange(0, QK_ROPE_DIM, 2, dtype=F32) / QK_ROPE_DIM)
    half = QK_ROPE_DIM // 2
    lane_id = np.arange(LANES)
    live = jnp.asarray(lane_id < 2 * QK_ROPE_DIM)
    sign = jnp.asarray(np.where((lane_id % QK_ROPE_DIM) < half, -1.0, 1.0).astype(np.float32))
    ang = positions.astype(F32)[..., None] * jnp.tile(inv_freq, LANES // half)
    cos_t = jnp.where(live, jnp.cos(ang), 0.0)
    sin_t = jnp.where(live, sign * jnp.sin(ang), 0.0)

    wts = (ffn_w_gate.astype(BF16), ffn_w_up.astype(BF16), ffn_w_down.astype(BF16),
           ln_g[:, :, None, :], ln_b[:, :, None, :])
    wo_all = w_o.astype(BF16)

    gate_pad = ((0, 0), (0, 0), (0, LANES - N_MIX_HEADS))
    fox_w = jnp.concatenate([
        fox_w_in[:, :, :mw] * (att_scale * LOG2E), fox_w_in[:, :, mw:2 * mw],
        jnp.pad(fox_w_in[:, :, 3 * mw:3 * mw + N_MIX_HEADS], gate_pad),
        fox_w_in[:, :, 3 * mw + N_MIX_HEADS:] * att_scale], axis=2).astype(BF16)
    fox_wvt = jnp.swapaxes(fox_w_in[:, :, 2 * mw:3 * mw], 1, 2).astype(BF16)
    fox_bf = jnp.pad(fox_b_f, ((0, 0), (0, LANES - N_MIX_HEADS)))[:, None, :]

    for i in range(depth):
        kind, jj = i % N_MIXERS, i // N_MIXERS
        if kind == 0:
            x, q, k, vt, qx, kx, mq = _ffn_fox_proj(x, wts, i, alpha, jj, fox_w, fox_wvt, fox_bf)
            main = _flash(q, qx, k, kx, vt, kb_shared=False,
                          b_lanes=(_fox_ext_lanes(0), _fox_ext_lanes(1)))
        elif kind == 1:
            w = pool_w_in[jj]
            w = jnp.concatenate([w[:, :mw], w[:, mw:] * att_scale], axis=1).astype(BF16)
            w_bd = jax.scipy.linalg.block_diag(*[pool_w_grp[jj, gidx]
                                                 for gidx in range(len(POOL_WINDOWS))]).astype(BF16)
            x, main, mq = _ffn_pool_proj(x, wts, i, alpha, w, w_bd, pool_scale[jj][None])
        else:
            w = mla_w_in[jj]
            o2 = Q_LORA_RANK + KV_LORA_RANK
            kr = w[:, o2:o2 + QK_ROPE_DIM]
            w = jnp.concatenate([w[:, :o2], kr, kr,
                                 jnp.zeros((d, LANES - 2 * QK_ROPE_DIM), F32),
                                 w[:, o2 + QK_ROPE_DIM:] * att_scale], axis=1).astype(BF16)
            qd = QK_NOPE_DIM + QK_ROPE_DIM
            wuq3 = mla_w_uq[jj].reshape(Q_LORA_RANK, N_MIX_HEADS, qd)
            wuq = jnp.concatenate([wuq3[:, :, :QK_NOPE_DIM].reshape(Q_LORA_RANK, mw),
                                   _pair_pad_cols(wuq3, QK_NOPE_DIM, QK_ROPE_DIM)],
                                  axis=1).astype(BF16)
            kvd = QK_NOPE_DIM + V_HEAD_DIM
            wukv3 = mla_w_ukv[jj].reshape(KV_LORA_RANK, N_MIX_HEADS, kvd)
            wuk = wukv3[:, :, :QK_NOPE_DIM].reshape(KV_LORA_RANK, mw).astype(BF16)
            wuvt = wukv3[:, :, QK_NOPE_DIM:].reshape(KV_LORA_RANK, mw).T.astype(BF16)
            x, qn, qpe, kn, kpe, vt, mq = _ffn_mla_proj(
                x, wts, i, alpha, w, mla_q_norm[jj][None], mla_kv_norm[jj][None], wuq, wuk, wuvt,
                cos_t, sin_t, float(qd ** -0.5) * LOG2E)
            rope_lanes = (tuple(range(QK_ROPE_DIM)), tuple(range(QK_ROPE_DIM, 2 * QK_ROPE_DIM)))
            main = _flash(qn, qpe, kn, kpe, vt, kb_shared=True, b_lanes=rope_lanes)
        x = _mix_ffn(main, mq, memkv, i, wo_all, x, wts, alpha)
    return x
```
